```python
import math
import jax, jax.numpy as jnp
from jax import lax
import numpy as np

D_MODEL = 2048
BATCH = 4
SEQ = 4096
DEPTH = 2
DEC_BATCH = 1
DEC_SEQ = 8192
PAST_LEN = 128

GRID_W = 64
ATT_HEADS = 8
ATT_KV_HEADS = 2
ATT_HEAD_DIM = 128
ROPE_AXIS_DIM = ATT_HEAD_DIM // 2
ROPE_THETA = 10000.0
Q_BLOCK = 128
QK_EPS = 1e-6
ML_HEADS = 4
ML_HEAD_DIM = 256
ML_CHUNK = 128
ATT_WIDTH = ATT_HEADS * ATT_HEAD_DIM
KV_WIDTH = ATT_KV_HEADS * ATT_HEAD_DIM
ML_WIDTH = ML_HEADS * ML_HEAD_DIM
N_ML_GATES = 4 * ML_HEADS
IN_SIZES = (ATT_WIDTH, KV_WIDTH, KV_WIDTH, ML_WIDTH, ML_WIDTH, ML_WIDTH, ML_WIDTH,
            N_ML_GATES, D_MODEL, D_MODEL)
IN_COLS = sum(IN_SIZES)
D_FF_DENSE = 5632
N_EXPERTS = 8
TOP_K = 2
D_FF_EXPERT = 7168
N_DENSE = (DEPTH + 1) // 2
N_MOE = DEPTH // 2
ALPHA = (2 * DEPTH) ** 0.25
BETA = (8 * DEPTH) ** -0.25
LN_EPS = 1e-5
FORGET_BIAS = 3.0

kernel_name = "hybrid_gqa_mlstm_deepnorm_encoder"


def layer_norm(x, g, b):
    xf = x.astype(jnp.float32)
    mu = jnp.mean(xf, axis=-1, keepdims=True)
    var = jnp.mean(jnp.square(xf - mu), axis=-1, keepdims=True)
    return ((xf - mu) * lax.rsqrt(var + LN_EPS) * g.astype(jnp.float32)
            + b.astype(jnp.float32)).astype(x.dtype)


def rms_norm_f32(x, g):
    xf = x.astype(jnp.float32)
    return xf * lax.rsqrt(jnp.mean(jnp.square(xf), axis=-1, keepdims=True) + QK_EPS) * g.astype(jnp.float32)


def axial_rope_angles(T):
    rows = T // GRID_W
    pos_r = jnp.repeat(jnp.arange(rows, dtype=jnp.float32), GRID_W)
    pos_c = jnp.tile(jnp.arange(GRID_W, dtype=jnp.float32), rows)
    inv = ROPE_THETA ** (-jnp.arange(0, ROPE_AXIS_DIM, 2, dtype=jnp.float32) / ROPE_AXIS_DIM)
    return pos_r[:, None] * inv, pos_c[:, None] * inv


def rope_segment(x, ang):
    cos = jnp.cos(ang)[:, None, :]
    sin = jnp.sin(ang)[:, None, :]
    x1, x2 = jnp.split(x, 2, axis=-1)
    return jnp.concatenate([x1 * cos - x2 * sin, x2 * cos + x1 * sin], axis=-1)


def apply_axial_rope(x, ang_r, ang_c):
    return jnp.concatenate([rope_segment(x[..., :ROPE_AXIS_DIM], ang_r),
                            rope_segment(x[..., ROPE_AXIS_DIM:], ang_c)], axis=-1)


def gqa_attention(q, k, v):
    B, T, H, hd = q.shape
    Hk = k.shape[2]
    G = H // Hk
    nQ = T // Q_BLOCK
    qb = q.reshape(B, nQ, Q_BLOCK, Hk, G, hd).transpose(1, 0, 3, 4, 2, 5)
    kt = k.transpose(0, 2, 1, 3)
    vt = v.transpose(0, 2, 1, 3)
    scale = hd ** -0.5

    def block(qi):
        s = jnp.einsum('bkgqd,bktd->bkgqt', qi, kt).astype(jnp.float32) * scale
        p = jax.nn.softmax(s, axis=-1).astype(vt.dtype)
        return jnp.einsum('bkgqt,bktd->bkgqd', p, vt)

    o = lax.map(block, qb)
    return o.transpose(1, 0, 4, 2, 3, 5).reshape(B, T, H * hd)


def mlstm_dir(q, k, v, i_pre, f_pre):
    B, H, T, dk = q.shape
    dv = v.shape[-1]
    L = ML_CHUNK
    N = T // L
    f32 = jnp.float32
    q = q.astype(f32).reshape(B, H, N, L, dk)
    k = (k.astype(f32) * (dk ** -0.5)).reshape(B, H, N, L, dk)
    v = v.astype(f32).reshape(B, H, N, L, dv)
    ig = i_pre.astype(f32).reshape(B, H, N, L)
    logf = jax.nn.log_sigmoid(f_pre.astype(f32)).reshape(B, H, N, L)
    b = jnp.cumsum(logf, axis=-1)
    g = b[..., -1]

    a = g[..., None] - b + ig
    m_loc = jnp.max(a, axis=-1)
    w_a = jnp.exp(a - m_loc[..., None])
    C_loc = jnp.einsum('bhnlk,bhnlv->bhnkv', k * w_a[..., None], v)
    n_loc = jnp.einsum('bhnl,bhnlk->bhnk', w_a, k)

    def step(carry, inp):
        C, n, m = carry
        Cl, nl, ml, gl = inp
        m_new = jnp.maximum(gl + m, ml)
        s_old = jnp.exp(gl + m - m_new)
        s_new = jnp.exp(ml - m_new)
        C2 = s_old[..., None, None] * C + s_new[..., None, None] * Cl
        n2 = s_old[..., None] * n + s_new[..., None] * nl
        return (C2, n2, m_new), (C, n, m)

    init = (jnp.zeros((B, H, dk, dv), f32), jnp.zeros((B, H, dk), f32), jnp.zeros((B, H), f32))
    xs = (jnp.moveaxis(C_loc, 2, 0), jnp.moveaxis(n_loc, 2, 0),
          jnp.moveaxis(m_loc, 2, 0), jnp.moveaxis(g, 2, 0))
    _, (C_prev, n_prev, m_prev) = lax.scan(step, init, xs)
    C_prev = jnp.moveaxis(C_prev, 0, 2)
    n_prev = jnp.moveaxis(n_prev, 0, 2)
    m_prev = jnp.moveaxis(m_prev, 0, 2)

    mask = jnp.tril(jnp.ones((L, L), dtype=bool))
    D = jnp.where(mask, b[..., :, None] - b[..., None, :] + ig[..., None, :], -jnp.inf)
    inter = b + m_prev[..., None]
    m_j = jnp.maximum(jnp.max(D, axis=-1), inter)
    S = jnp.einsum('bhnjd,bhnld->bhnjl', q, k) * jnp.exp(D - m_j[..., None])
    s_int = jnp.exp(inter - m_j)
    num = (jnp.einsum('bhnjl,bhnlv->bhnjv', S, v)
           + s_int[..., None] * jnp.einsum('bhnjk,bhnkv->bhnjv', q, C_prev))
    den = jnp.sum(S, axis=-1) + s_int * jnp.einsum('bhnjk,bhnk->bhnj', q, n_prev)
    h = num / jnp.maximum(jnp.abs(den), jnp.exp(-m_j))[..., None]
    return h.reshape(B, H, T, dv)


def token_mixer(x, w_in, b_gates, q_gain, k_gain, w_att_br, w_ml_br, w_o):
    B, T, _ = x.shape
    u = x @ w_in
    split_points = [int(s) for s in np.cumsum(np.array(IN_SIZES))[:-1]]
    aq, ak, av, mq, mk, mv, mo, mg, ga, gm = jnp.split(u, split_points, axis=-1)

    ang_r, ang_c = axial_rope_angles(T)
    q = rms_norm_f32(aq.reshape(B, T, ATT_HEADS, ATT_HEAD_DIM), q_gain)
    k = rms_norm_f32(ak.reshape(B, T, ATT_KV_HEADS, ATT_HEAD_DIM), k_gain)
    q = apply_axial_rope(q, ang_r, ang_c).astype(x.dtype)
    k = apply_axial_rope(k, ang_r, ang_c).astype(x.dtype)
    a_out = gqa_attention(q, k, av.reshape(B, T, ATT_KV_HEADS, ATT_HEAD_DIM))

    def heads(t):
        return t.reshape(B, T, ML_HEADS, ML_HEAD_DIM).transpose(0, 2, 1, 3)
    gates = (mg.astype(jnp.float32) + b_gates.astype(jnp.float32)).transpose(0, 2, 1)
    i_f, i_b, f_f, f_b = jnp.split(gates, 4, axis=1)
    qh, kh, vh = heads(mq), heads(mk), heads(mv)
    rev = lambda t: jnp.flip(t, axis=2)
    h_fwd = mlstm_dir(qh, kh, vh, i_f, f_f)
    h_bwd = rev(mlstm_dir(rev(qh), rev(kh), rev(vh), rev(i_b), rev(f_b)))
    h = (h_fwd + h_bwd).transpose(0, 2, 1, 3).reshape(B, T, ML_WIDTH)
    m_out = (jax.nn.sigmoid(mo.astype(jnp.float32)) * h).astype(x.dtype)

    merged = jax.nn.sigmoid(ga) * (a_out @ w_att_br) + jax.nn.sigmoid(gm) * (m_out @ w_ml_br)
    return merged @ w_o


def swiglu(x, w1, w3, w2):
    return (jax.nn.silu(x @ w1) * (x @ w3)) @ w2


def moe_swiglu(x, router, w1, w3, w2):
    B, T, D = x.shape
    xt = x.reshape(B * T, D)
    logits = (xt @ router).astype(jnp.float32)
    top_v, top_i = lax.top_k(logits, TOP_K)
    top_w = jax.nn.softmax(top_v, axis=-1)
    gate = jnp.sum(jax.nn.one_hot(top_i, N_EXPERTS, dtype=jnp.float32) * top_w[..., None], axis=1)
    out = jnp.zeros((B * T, D), jnp.float32)
    for e in range(N_EXPERTS):
        out = out + gate[:, e:e + 1] * swiglu(xt, w1[e], w3[e], w2[e]).astype(jnp.float32)
    return out.astype(x.dtype).reshape(B, T, D)


def trunk(x, w_in, b_gates, q_gain, k_gain, w_att_br, w_ml_br, w_o, ln1_g, ln1_b,
          w1_d, w3_d, w2_d, router, e_w1, e_w3, e_w2, ln2_g, ln2_b):
    for l in range(DEPTH):
        y = token_mixer(x, w_in[l], b_gates[l], q_gain[l], k_gain[l], w_att_br[l], w_ml_br[l], w_o[l])
        x = layer_norm(ALPHA * x + y, ln1_g[l], ln1_b[l])
        if l % 2 == 0:
            j = l // 2
            f = swiglu(x, w1_d[j], w3_d[j], w2_d[j])
        else:
            j = l // 2
            f = moe_swiglu(x, router[j], e_w1[j], e_w3[j], e_w2[j])
        x = layer_norm(ALPHA * x + f, ln2_g[l], ln2_b[l])
    return x


def setup_inputs(seed: int = 0) -> dict:
    key = jax.random.key(seed)
    ks = jax.random.split(key, 24)
    f32 = jnp.float32

    def nrm(k, shape, scale):
        return jax.random.normal(k, shape, f32) * scale

    x_prompt = nrm(ks[0], (BATCH, SEQ, D_MODEL), 1.0)
    x_sample = nrm(ks[1], (DEC_BATCH, DEC_SEQ, D_MODEL), 1.0)
    w_in = nrm(ks[2], (DEPTH, D_MODEL, IN_COLS), D_MODEL ** -0.5)
    i_bias = nrm(ks[3], (DEPTH, 2 * ML_HEADS), 0.1)
    f_bias = FORGET_BIAS + nrm(ks[4], (DEPTH, 2 * ML_HEADS), 0.1)
    b_gates = jnp.concatenate([i_bias, f_bias], axis=-1)
    q_gain = 1.0 + nrm(ks[5], (DEPTH, ATT_HEAD_DIM), 0.02)
    k_gain = 1.0 + nrm(ks[6], (DEPTH, ATT_HEAD_DIM), 0.02)
    w_att_br = nrm(ks[7], (DEPTH, ATT_WIDTH, D_MODEL), ATT_WIDTH ** -0.5)
    w_ml_br = nrm(ks[8], (DEPTH, ML_WIDTH, D_MODEL), ML_WIDTH ** -0.5)
    w_o = nrm(ks[9], (DEPTH, D_MODEL, D_MODEL), BETA * D_MODEL ** -0.5)
    ln1_g = 1.0 + nrm(ks[10], (DEPTH, D_MODEL), 0.02)
    ln1_b = nrm(ks[11], (DEPTH, D_MODEL), 0.02)
    w1_d = nrm(ks[12], (N_DENSE, D_MODEL, D_FF_DENSE), D_MODEL ** -0.5)
    w3_d = nrm(ks[13], (N_DENSE, D_MODEL, D_FF_DENSE), D_MODEL ** -0.5)
    w2_d = nrm(ks[14], (N_DENSE, D_FF_DENSE, D_MODEL), BETA * D_FF_DENSE ** -0.5)
    router = nrm(ks[15], (N_MOE, D_MODEL, N_EXPERTS), D_MODEL ** -0.5)
    e_w1 = nrm(ks[16], (N_MOE, N_EXPERTS, D_MODEL, D_FF_EXPERT), D_MODEL ** -0.5)
    e_w3 = nrm(ks[17], (N_MOE, N_EXPERTS, D_MODEL, D_FF_EXPERT), D_MODEL ** -0.5)
    e_w2 = nrm(ks[18], (N_MOE, N_EXPERTS, D_FF_EXPERT, D_MODEL), BETA * D_FF_EXPERT ** -0.5)
    ln2_g = 1.0 + nrm(ks[19], (DEPTH, D_MODEL), 0.02)
    ln2_b = nrm(ks[20], (DEPTH, D_MODEL), 0.02)
    return {"x_prompt": x_prompt, "x_sample": x_sample, "w_in": w_in, "b_gates": b_gates,
            "q_gain": q_gain, "k_gain": k_gain, "w_att_br": w_att_br, "w_ml_br": w_ml_br,
            "w_o": w_o, "ln1_g": ln1_g, "ln1_b": ln1_b, "w1_d": w1_d, "w3_d": w3_d, "w2_d": w2_d,
            "router": router, "e_w1": e_w1, "e_w3": e_w3, "e_w2": e_w2,
            "ln2_g": ln2_g, "ln2_b": ln2_b}


def reference(x_prompt, x_sample, w_in, b_gates, q_gain, k_gain, w_att_br, w_ml_br, w_o,
              ln1_g, ln1_b, w1_d, w3_d, w2_d, router, e_w1, e_w3, e_w2, ln2_g, ln2_b):
    y_prompt = trunk(x_prompt, w_in, b_gates, q_gain, k_gain, w_att_br, w_ml_br, w_o, ln1_g, ln1_b,
                     w1_d, w3_d, w2_d, router, e_w1, e_w3, e_w2, ln2_g, ln2_b)
    y_sample = trunk(x_sample, w_in, b_gates, q_gain, k_gain, w_att_br, w_ml_br, w_o, ln1_g, ln1_b,
                     w1_d, w3_d, w2_d, router, e_w1, e_w3, e_w2, ln2_g, ln2_b)
    return (y_prompt, y_sample)
```

```python
import functools
import math

import numpy as np
import jax
import jax.numpy as jnp
from jax import lax
from jax.experimental import pallas as pl
from jax.experimental.pallas import tpu as pltpu

F32 = jnp.float32
BF16 = jnp.bfloat16
I32 = jnp.int32

GRID_W = 64
ATT_HEADS = 8
ATT_KV_HEADS = 2
ATT_HEAD_DIM = 128
ATT_GROUP = ATT_HEADS // ATT_KV_HEADS
ROPE_THETA = 10000.0
QK_EPS = 1e-6
ML_HEADS = 4
ML_HEAD_DIM = 256
ML_CHUNK = 128
ATT_WIDTH = ATT_HEADS * ATT_HEAD_DIM
KV_WIDTH = ATT_KV_HEADS * ATT_HEAD_DIM
ML_WIDTH = ML_HEADS * ML_HEAD_DIM
N_ML_GATES = 4 * ML_HEADS
N_EXPERTS = 8
TOP_K = 2
LN_EPS = 1e-5

LANES = 128
V7X_VMEM_BYTES = 64 * 2 ** 20
MIB = 2 ** 20


def _cparams(semantics, vmem_mib):
    assert vmem_mib * MIB < V7X_VMEM_BYTES
    return pltpu.CompilerParams(dimension_semantics=semantics, vmem_limit_bytes=vmem_mib * MIB)


def _sigmoid(x):
    return 1.0 / (1.0 + jnp.exp(-x))


def _log_sigmoid(x):
    return jnp.minimum(x, 0.0) - jnp.log1p(jnp.exp(-jnp.abs(x)))


def _mean_last(z, n_axes):
    count = 1
    for ax in range(z.ndim - 1, z.ndim - 1 - n_axes, -1):
        count *= z.shape[ax]
        z = jnp.sum(z, axis=ax, keepdims=True)
    return z * (1.0 / count)


def _layer_norm(z, g, b, n_axes=1):
    mu = _mean_last(z, n_axes)
    zc = z - mu
    var = _mean_last(zc * zc, n_axes)
    return zc * lax.rsqrt(var + LN_EPS) * g + b


def _att_proj_kernel(x_ref, w_ref, bias_ref, gain_ref, cos_ref, sa_ref, sb_ref,
                     qkv_ref, gates_ref, acc_ref, *, n_heads, n_norm_heads):
    acc_ref[...] = jnp.dot(x_ref[...], w_ref[...], preferred_element_type=F32)
    cos = cos_ref[...]
    sa = sa_ref[...]
    sb = sb_ref[...]
    for h in range(n_heads):
        a = acc_ref[:, h * LANES:(h + 1) * LANES]
        if h < n_norm_heads:
            a = a * lax.rsqrt(jnp.mean(a * a, axis=-1, keepdims=True) + QK_EPS) * gain_ref[h:h + 1, :]
            a = a * cos + pltpu.roll(a, 96, 1) * sa + pltpu.roll(a, 32, 1) * sb
        qkv_ref[:, h * LANES:(h + 1) * LANES] = a.astype(qkv_ref.dtype)
    gates_ref[...] = acc_ref[:, n_heads * LANES:] + bias_ref[...]


def _att_proj(xb, w, bias, gain, cos, sa, sb, seqs, tm):
    n, d = xb.shape
    n_heads = (ATT_WIDTH + 2 * KV_WIDTH) // LANES
    ncol = n_heads * LANES + LANES
    assert w.shape == (d, ncol)

    (b0, t0), (b1, t1) = seqs
    assert t0 % tm == 0 and t1 % tm == 0
    nb0 = b0 * t0 // tm

    def pos_map(i):
        return (jnp.where(i < nb0, i % (t0 // tm), (i - nb0) % (t1 // tm)), 0)

    return pl.pallas_call(
        functools.partial(_att_proj_kernel, n_heads=n_heads, n_norm_heads=ATT_HEADS + ATT_KV_HEADS),
        out_shape=(jax.ShapeDtypeStruct((n, n_heads * LANES), BF16),
                   jax.ShapeDtypeStruct((n, LANES), F32)),
        grid=(n // tm,),
        in_specs=[pl.BlockSpec((tm, d), lambda i: (i, 0)),
                  pl.BlockSpec((d, ncol), lambda i: (0, 0)),
                  pl.BlockSpec((1, LANES), lambda i: (0, 0)),
                  pl.BlockSpec((16, LANES), lambda i: (0, 0)),
                  pl.BlockSpec((tm, LANES), pos_map),
                  pl.BlockSpec((tm, LANES), pos_map),
                  pl.BlockSpec((tm, LANES), pos_map)],
        out_specs=(pl.BlockSpec((tm, n_heads * LANES), lambda i: (i, 0)),
                   pl.BlockSpec((tm, LANES), lambda i: (i, 0))),
        scratch_shapes=[pltpu.VMEM((tm, ncol), F32)],
        compiler_params=_cparams(("parallel",), 48),
        name="att_proj",
    )(xb, w, bias, gain, cos, sa, sb)


def _matmul_kernel(x_ref, w_ref, o_ref):
    o_ref[...] = jnp.dot(x_ref[...], w_ref[...], preferred_element_type=F32).astype(o_ref.dtype)


def _matmul(xb, w, out_dtype, tm, tn):
    n, d = xb.shape
    ncol = w.shape[1]
    assert n % tm == 0 and ncol % tn == 0
    return pl.pallas_call(
        _matmul_kernel,
        out_shape=jax.ShapeDtypeStruct((n, ncol), out_dtype),
        grid=(ncol // tn, n // tm),
        in_specs=[pl.BlockSpec((tm, d), lambda j, i: (i, 0)),
                  pl.BlockSpec((d, tn), lambda j, i: (0, j))],
        out_specs=pl.BlockSpec((tm, tn), lambda j, i: (i, j)),
        compiler_params=_cparams(("parallel", "parallel"), 48),
        name="matmul",
    )(xb, w)


def _attn_kernel(q_ref, k_ref, v_ref, o_ref, qs_ref, m_ref, l_ref, acc_ref, *, tk, scale):
    bq = q_ref.shape[0]
    t = k_ref.shape[0]
    for h in range(ATT_GROUP):
        qs_ref[h * bq:(h + 1) * bq, :] = q_ref[:, h * LANES:(h + 1) * LANES]
    m_ref[...] = jnp.full(m_ref.shape, -jnp.inf, F32)
    l_ref[...] = jnp.zeros(l_ref.shape, F32)
    acc_ref[...] = jnp.zeros(acc_ref.shape, F32)

    def body(c, carry):
        start = pl.multiple_of(c * tk, tk)
        k = k_ref[pl.ds(start, tk), :]
        v = v_ref[pl.ds(start, tk), :]
        s = lax.dot_general(qs_ref[...], k, (((1,), (1,)), ((), ())),
                            preferred_element_type=F32) * scale
        m_prev = m_ref[...]
        m_new = jnp.maximum(m_prev, jnp.max(s, axis=1, keepdims=True))
        p = jnp.exp(s - m_new)
        alpha = jnp.exp(m_prev - m_new)
        l_ref[...] = alpha * l_ref[...] + jnp.sum(p, axis=1, keepdims=True)
        acc_ref[...] = alpha * acc_ref[...] + jnp.dot(p.astype(BF16), v, preferred_element_type=F32)
        m_ref[...] = m_new
        return carry

    lax.fori_loop(0, t // tk, body, 0)
    for h in range(ATT_GROUP):
        inv = 1.0 / l_ref[h * bq:(h + 1) * bq, :]
        o_ref[:, h * LANES:(h + 1) * LANES] = (acc_ref[h * bq:(h + 1) * bq, :] * inv).astype(o_ref.dtype)


def _attention(qkv, tok0, nseq, t, bq, tk):
    assert t % bq == 0 and t % tk == 0 and tok0 % t == 0
    qb0 = tok0 // bq
    sb0 = tok0 // t
    nq = t // bq
    gw = ATT_GROUP * LANES
    return pl.pallas_call(
        functools.partial(_attn_kernel, tk=tk, scale=ATT_HEAD_DIM ** -0.5),
        out_shape=jax.ShapeDtypeStruct((nseq * t, ATT_WIDTH), BF16),
        grid=(nseq, ATT_KV_HEADS, nq),
        in_specs=[pl.BlockSpec((bq, gw), lambda b, g, i: (qb0 + b * nq + i, g)),
                  pl.BlockSpec((t, LANES), lambda b, g, i: (sb0 + b, ATT_HEADS + g)),
                  pl.BlockSpec((t, LANES), lambda b, g, i: (sb0 + b, ATT_HEADS + ATT_KV_HEADS + g))],
        out_specs=pl.BlockSpec((bq, gw), lambda b, g, i: (b * nq + i, g)),
        scratch_shapes=[pltpu.VMEM((ATT_GROUP * bq, LANES), BF16),
                        pltpu.VMEM((ATT_GROUP * bq, 1), F32),
                        pltpu.VMEM((ATT_GROUP * bq, 1), F32),
                        pltpu.VMEM((ATT_GROUP * bq, LANES), F32)],
        compiler_params=_cparams(("parallel", "parallel", "parallel"), 40),
        name="attention",
    )(qkv, qkv, qkv)


def _mlstm_kernel(blk_ref, first_ref, q_ref, k_ref, v_ref, g_ref, gt_ref, h_ref, c_ref, n_ref, m_ref):
    d = pl.program_id(0)
    hd = pl.program_id(1)
    i = pl.program_id(2)
    L = q_ref.shape[0]
    dk = q_ref.shape[1]

    @pl.when(first_ref[d, i] == 1)
    def _():
        c_ref[...] = jnp.zeros(c_ref.shape, F32)
        n_ref[...] = jnp.zeros(n_ref.shape, F32)
        m_ref[...] = jnp.zeros(m_ref.shape, F32)

    ci = d * ML_HEADS + hd
    cf = 2 * ML_HEADS + ci
    g = g_ref[...]
    gt = gt_ref[...]
    lane = lax.broadcasted_iota(I32, g.shape, 1)
    sub = lax.broadcasted_iota(I32, gt.shape, 0)
    ig_col = jnp.sum(jnp.where(lane == ci, g, 0.0), axis=1, keepdims=True)
    fp_col = jnp.sum(jnp.where(lane == cf, g, 0.0), axis=1, keepdims=True)
    ig_row = jnp.sum(jnp.where(sub == ci, gt, 0.0), axis=0, keepdims=True)
    fp_row = jnp.sum(jnp.where(sub == cf, gt, 0.0), axis=0, keepdims=True)
    logf_col = _log_sigmoid(fp_col)
    logf_row = _log_sigmoid(fp_row)

    row = lax.broadcasted_iota(I32, (L, L), 0)
    col = lax.broadcasted_iota(I32, (L, L), 1)
    sgn = 1 - 2 * d
    mask = (col - row) * sgn <= 0
    b_col = jnp.sum(jnp.where(mask, logf_row, 0.0), axis=1, keepdims=True)
    mask_t = (row - col) * sgn <= 0
    b_row = jnp.sum(jnp.where(mask_t, logf_col, 0.0), axis=0, keepdims=True)
    gtot = jnp.sum(logf_row, axis=1, keepdims=True)

    m_prev = m_ref[...]
    q = q_ref[...]
    v = v_ref[...]
    ks = k_ref[...].astype(F32) * (dk ** -0.5)

    a_row = gtot - b_row + ig_row
    a_col = gtot - b_col + ig_col
    m_loc = jnp.max(a_row, axis=1, keepdims=True)
    kw = ks * jnp.exp(a_col - m_loc)
    c_loc = lax.dot_general(kw.astype(BF16), v, (((0,), (0,)), ((), ())), preferred_element_type=F32)
    n_loc = jnp.sum(kw, axis=0, keepdims=True)

    dmat = jnp.where(mask, b_col - b_row + ig_row, -jnp.inf)
    inter = b_col + m_prev
    m_j = jnp.maximum(jnp.max(dmat, axis=1, keepdims=True), inter)
    qk = lax.dot_general(q, ks.astype(BF16), (((1,), (1,)), ((), ())), preferred_element_type=F32)
    s = qk * jnp.exp(dmat - m_j)
    s_int = jnp.exp(inter - m_j)
    num = (jnp.dot(s.astype(BF16), v, preferred_element_type=F32)
           + s_int * jnp.dot(q, c_ref[...].astype(BF16), preferred_element_type=F32))
    den = (jnp.sum(s, axis=1, keepdims=True)
           + s_int * jnp.sum(q.astype(F32) * n_ref[...], axis=1, keepdims=True))
    h_ref[...] = num / jnp.maximum(jnp.abs(den), jnp.exp(-m_j))

    m_new = jnp.maximum(gtot + m_prev, m_loc)
    s_old = jnp.exp(gtot + m_prev - m_new)
    s_new = jnp.exp(m_loc - m_new)
    c_ref[...] = s_old * c_ref[...] + s_new * c_loc
    n_ref[...] = s_old * n_ref[...] + s_new * n_loc
    m_ref[...] = m_new


def _mlstm(rest, gates, gates_t, seqs):
    n = rest.shape[0]
    L = ML_CHUNK
    nc = n // L
    first_f = np.zeros((nc,), np.int32)
    last_f = np.zeros((nc,), np.int32)
    tok = 0
    for (b, t) in seqs:
        for _ in range(b):
            first_f[tok // L] = 1
            last_f[(tok + t) // L - 1] = 1
            tok += t
    blk = np.stack([np.arange(nc), np.arange(nc)[::-1]]).astype(np.int32)
    first = np.stack([first_f, last_f[::-1]]).astype(np.int32)
    H = ML_HEADS

    grid_spec = pltpu.PrefetchScalarGridSpec(
        num_scalar_prefetch=2,
        grid=(2, H, nc),
        in_specs=[pl.BlockSpec((L, ML_HEAD_DIM), lambda d, h, i, blk, fst: (blk[d, i], h)),
                  pl.BlockSpec((L, ML_HEAD_DIM), lambda d, h, i, blk, fst: (blk[d, i], H + h)),
                  pl.BlockSpec((L, ML_HEAD_DIM), lambda d, h, i, blk, fst: (blk[d, i], 2 * H + h)),
                  pl.BlockSpec((L, LANES), lambda d, h, i, blk, fst: (blk[d, i], 0)),
                  pl.BlockSpec((N_ML_GATES, L), lambda d, h, i, blk, fst: (0, blk[d, i]))],
        out_specs=pl.BlockSpec((None, L, ML_HEAD_DIM), lambda d, h, i, blk, fst: (d, blk[d, i], h)),
        scratch_shapes=[pltpu.VMEM((ML_HEAD_DIM, ML_HEAD_DIM), F32),
                        pltpu.VMEM((1, ML_HEAD_DIM), F32),
                        pltpu.VMEM((1, 1), F32)],
    )
    return pl.pallas_call(
        _mlstm_kernel,
        out_shape=jax.ShapeDtypeStruct((2, n, ML_WIDTH), F32),
        grid_spec=grid_spec,
        compiler_params=_cparams(("parallel", "parallel", "arbitrary"), 32),
        name="mlstm",
    )(jnp.asarray(blk), jnp.asarray(first), rest, rest, rest, gates, gates_t)


def _merge_kernel(a_ref, hf_ref, hb_ref, mo_ref, ga_ref, gm_ref, x_ref, wa_ref, wm_ref, wo_ref,
                  lng_ref, lnb_ref, xo_ref, xob_ref, *, alpha):
    m_out = (_sigmoid(mo_ref[...].astype(F32)) * (hf_ref[...] + hb_ref[...])).astype(BF16)
    pa = jnp.dot(a_ref[...], wa_ref[...], preferred_element_type=F32)
    pm = jnp.dot(m_out, wm_ref[...], preferred_element_type=F32)
    merged = _sigmoid(ga_ref[...].astype(F32)) * pa + _sigmoid(gm_ref[...].astype(F32)) * pm
    y = jnp.dot(merged.astype(BF16), wo_ref[...], preferred_element_type=F32)
    z = _layer_norm(alpha * x_ref[...] + y, lng_ref[...], lnb_ref[...])
    xo_ref[...] = z
    xob_ref[...] = z.astype(BF16)


def _merge(a_out, h, rest, x, wa, wm, wo, lng, lnb, alpha, tm):
    n, d = x.shape
    assert n % tm == 0 and d == 2 * ML_WIDTH and ATT_WIDTH == ML_WIDTH
    const = dict(pipeline_mode=pl.Buffered(1))
    return pl.pallas_call(
        functools.partial(_merge_kernel, alpha=alpha),
        out_shape=(jax.ShapeDtypeStruct((n, d), F32), jax.ShapeDtypeStruct((n, d), BF16)),
        grid=(n // tm,),
        in_specs=[pl.BlockSpec((tm, ATT_WIDTH), lambda i: (i, 0)),
                  pl.BlockSpec((None, tm, ML_WIDTH), lambda i: (0, i, 0)),
                  pl.BlockSpec((None, tm, ML_WIDTH), lambda i: (1, i, 0)),
                  pl.BlockSpec((tm, ML_WIDTH), lambda i: (i, 3)),
                  pl.BlockSpec((tm, d), lambda i: (i, 2)),
                  pl.BlockSpec((tm, d), lambda i: (i, 3)),
                  pl.BlockSpec((tm, d), lambda i: (i, 0)),
                  pl.BlockSpec((ATT_WIDTH, d), lambda i: (0, 0), **const),
                  pl.BlockSpec((ML_WIDTH, d), lambda i: (0, 0), **const),
                  pl.BlockSpec((d, d), lambda i: (0, 0), **const),
                  pl.BlockSpec((1, d), lambda i: (0, 0)),
                  pl.BlockSpec((1, d), lambda i: (0, 0))],
        out_specs=(pl.BlockSpec((tm, d), lambda i: (i, 0)),
                   pl.BlockSpec((tm, d), lambda i: (i, 0))),
        compiler_params=_cparams(("parallel",), 56),
        name="merge",
    )(a_out, h, h, rest, rest, rest, x, wa, wm, wo, lng, lnb)


def _ffn_kernel(xb_ref, w1_ref, w3_ref, w2_ref, x_ref, lng_ref, lnb_ref, xo_ref, xob_ref, acc_ref,
                *, alpha):
    j = pl.program_id(1)

    @pl.when(j == 0)
    def _():
        acc_ref[...] = jnp.zeros(acc_ref.shape, F32)

    xb = xb_ref[...]
    a = jnp.dot(xb, w1_ref[...], preferred_element_type=F32)
    b = jnp.dot(xb, w3_ref[...], preferred_element_type=F32)
    hid = (a * _sigmoid(a) * b).astype(BF16)
    acc_ref[...] += jnp.dot(hid, w2_ref[...], preferred_element_type=F32)

    @pl.when(j == pl.num_programs(1) - 1)
    def _():
        z = _layer_norm(alpha * x_ref[...] + acc_ref[...], lng_ref[...], lnb_ref[...])
        xo_ref[...] = z
        xob_ref[...] = z.astype(BF16)


def _ffn(xb, x, w1, w3, w2, lng, lnb, alpha, tm, tf):
    n, d = x.shape
    f = w1.shape[1]
    assert n % tm == 0 and f % tf == 0
    return pl.pallas_call(
        functools.partial(_ffn_kernel, alpha=alpha),
        out_shape=(jax.ShapeDtypeStruct((n, d), F32), jax.ShapeDtypeStruct((n, d), BF16)),
        grid=(n // tm, f // tf),
        in_specs=[pl.BlockSpec((tm, d), lambda i, j: (i, 0)),
                  pl.BlockSpec((d, tf), lambda i, j: (0, j)),
                  pl.BlockSpec((d, tf), lambda i, j: (0, j)),
                  pl.BlockSpec((tf, d), lambda i, j: (j, 0)),
                  pl.BlockSpec((tm, d), lambda i, j: (i, 0)),
                  pl.BlockSpec((1, d), lambda i, j: (0, 0)),
                  pl.BlockSpec((1, d), lambda i, j: (0, 0))],
        out_specs=(pl.BlockSpec((tm, d), lambda i, j: (i, 0)),
                   pl.BlockSpec((tm, d), lambda i, j: (i, 0))),
        scratch_shapes=[pltpu.VMEM((tm, d), F32)],
        compiler_params=_cparams(("parallel", "arbitrary"), 56),
        name="ffn_dense",
    )(xb, w1, w3, w2, x, lng, lnb)


def _router_kernel(x_ref, wr_ref, tri_ref, ints_ref, flts_ref, cnt_ref, base_ref):
    i = pl.program_id(0)

    @pl.when(i == 0)
    def _():
        base_ref[...] = jnp.zeros(base_ref.shape, F32)

    logits = jnp.dot(x_ref[...], wr_ref[...], preferred_element_type=F32,
                     precision=lax.Precision.HIGHEST)
    lane = lax.broadcasted_iota(I32, logits.shape, 1)
    lanef = lane.astype(F32)
    logits = jnp.where(lane < N_EXPERTS, logits, -jnp.inf)
    v0 = jnp.max(logits, axis=1, keepdims=True)
    i0 = jnp.min(jnp.where(logits == v0, lanef, float(LANES)), axis=1, keepdims=True)
    rest = jnp.where(lanef == i0, -jnp.inf, logits)
    v1 = jnp.max(rest, axis=1, keepdims=True)
    i1 = jnp.min(jnp.where(rest == v1, lanef, float(LANES)), axis=1, keepdims=True)
    e1 = jnp.exp(v1 - v0)
    g0 = 1.0 / (1.0 + e1)
    g1 = e1 / (1.0 + e1)

    sel0 = lanef == i0
    sel1 = lanef == i1
    onehot = jnp.where(sel0 | sel1, 1.0, 0.0)
    before = jnp.dot(tri_ref[...], onehot.astype(BF16), preferred_element_type=F32) + base_ref[...]
    r0 = jnp.sum(jnp.where(sel0, before, 0.0), axis=1, keepdims=True)
    r1 = jnp.sum(jnp.where(sel1, before, 0.0), axis=1, keepdims=True)
    base_ref[...] += jnp.sum(onehot, axis=0, keepdims=True)
    cnt_ref[...] = jnp.broadcast_to(base_ref[...], cnt_ref.shape)

    ints = jnp.where(lane == 0, i0, jnp.where(lane == 1, i1, jnp.where(lane == 2, r0,
                     jnp.where(lane == 3, r1, 0.0))))
    ints_ref[...] = ints.astype(I32)
    flts_ref[...] = jnp.where(lane == 0, g0, jnp.where(lane == 1, g1, 0.0))


def _router(x, wr, tr):
    n, d = x.shape
    assert n % tr == 0
    tri = jnp.asarray(np.tril(np.ones((tr, tr), np.float32), -1), BF16)
    return pl.pallas_call(
        _router_kernel,
        out_shape=(jax.ShapeDtypeStruct((n, LANES), I32),
                   jax.ShapeDtypeStruct((n, LANES), F32),
                   jax.ShapeDtypeStruct((8, LANES), F32)),
        grid=(n // tr,),
        in_specs=[pl.BlockSpec((tr, d), lambda i: (i, 0)),
                  pl.BlockSpec((d, LANES), lambda i: (0, 0)),
                  pl.BlockSpec((tr, tr), lambda i: (0, 0))],
        out_specs=(pl.BlockSpec((tr, LANES), lambda i: (i, 0)),
                   pl.BlockSpec((tr, LANES), lambda i: (i, 0)),
                   pl.BlockSpec((8, LANES), lambda i: (0, 0))),
        scratch_shapes=[pltpu.VMEM((1, LANES), F32)],
        compiler_params=_cparams(("arbitrary",), 32),
        name="moe_router",
    )(x, wr, tri)


def _dispatch_kernel(pos_ref, x_ref, xs_in_ref, xs_ref, sem):
    del xs_in_ref
    tt = x_ref.shape[0]

    def copy(t, k):
        return pltpu.make_async_copy(x_ref.at[t], xs_ref.at[pos_ref[0, k, t]], sem)

    def issue(t, carry):
        copy(t, 0).start()
        copy(t, 1).start()
        return carry

    def drain(t, carry):
        copy(t, 0).wait()
        copy(t, 1).wait()
        return carry

    lax.fori_loop(0, tt, issue, 0)
    lax.fori_loop(0, tt, drain, 0)


def _dispatch(x3, pos, n_rows, tt):
    n, s, _ = x3.shape
    xs0 = jnp.zeros((n_rows, s, LANES), x3.dtype)
    return pl.pallas_call(
        _dispatch_kernel,
        out_shape=jax.ShapeDtypeStruct(xs0.shape, xs0.dtype),
        grid=(n // tt,),
        in_specs=[pl.BlockSpec((1, TOP_K, tt), lambda i: (i, 0, 0), memory_space=pltpu.SMEM),
                  pl.BlockSpec((tt, s, LANES), lambda i: (i, 0, 0)),
                  pl.BlockSpec(memory_space=pl.ANY)],
        out_specs=pl.BlockSpec(memory_space=pl.ANY),
        scratch_shapes=[pltpu.SemaphoreType.DMA],
        input_output_aliases={2: 0},
        compiler_params=_cparams(("arbitrary",), 32),
        name="moe_dispatch",
    )(pos, x3, xs0)


def _moe_ffn_kernel(te_ref, nv_ref, xs_ref, w1_ref, w3_ref, w2_ref, y_ref):
    i = pl.program_id(0)
    j = pl.program_id(1)

    @pl.when(i < nv_ref[0])
    def _():
        @pl.when(j == 0)
        def _():
            y_ref[...] = jnp.zeros(y_ref.shape, F32)

        xb = xs_ref[...]
        a = jnp.dot(xb, w1_ref[...], preferred_element_type=F32)
        b = jnp.dot(xb, w3_ref[...], preferred_element_type=F32)
        hid = (a * _sigmoid(a) * b).astype(BF16)
        y_ref[...] += jnp.dot(hid, w2_ref[...], preferred_element_type=F32)

    @pl.when((i >= nv_ref[0]) & (j == 0))
    def _():
        y_ref[...] = jnp.zeros(y_ref.shape, F32)


def _moe_ffn(xs, tile_expert, n_valid, w1, w3, w2, tm, tf):
    p, d = xs.shape
    f = w1.shape[2]
    assert p % tm == 0 and f % tf == 0
    nj = f // tf

    def row_map(i, j, te, nv):
        return (jnp.minimum(i, nv[0] - 1), 0)

    def jj(i, j, nv):
        return jnp.where(i < nv[0], j, nj - 1)

    grid_spec = pltpu.PrefetchScalarGridSpec(
        num_scalar_prefetch=2,
        grid=(p // tm, nj),
        in_specs=[pl.BlockSpec((tm, d), row_map),
                  pl.BlockSpec((None, d, tf), lambda i, j, te, nv: (te[i], 0, jj(i, j, nv))),
                  pl.BlockSpec((None, d, tf), lambda i, j, te, nv: (te[i], 0, jj(i, j, nv))),
                  pl.BlockSpec((None, tf, d), lambda i, j, te, nv: (te[i], jj(i, j, nv), 0))],
        out_specs=pl.BlockSpec((tm, d), lambda i, j, te, nv: (i, 0)),
    )
    return pl.pallas_call(
        _moe_ffn_kernel,
        out_shape=jax.ShapeDtypeStruct((p, d), F32),
        grid_spec=grid_spec,
        compiler_params=_cparams(("arbitrary", "arbitrary"), 56),
        name="moe_ffn",
    )(tile_expert, n_valid, xs, w1, w3, w2)


def _combine_kernel(pos_ref, y_ref, g0_ref, g1_ref, x_ref, lng_ref, lnb_ref, xo_ref,
                    buf0_ref, buf1_ref, sem, *, alpha):
    tt = x_ref.shape[0]

    def copy(t, k):
        buf = buf0_ref if k == 0 else buf1_ref
        return pltpu.make_async_copy(y_ref.at[pos_ref[0, k, t]], buf.at[t], sem)

    def issue(t, carry):
        copy(t, 0).start()
        copy(t, 1).start()
        return carry

    def drain(t, carry):
        copy(t, 0).wait()
        copy(t, 1).wait()
        return carry

    lax.fori_loop(0, tt, issue, 0)
    lax.fori_loop(0, tt, drain, 0)
    f = g0_ref[...] * buf0_ref[...] + g1_ref[...] * buf1_ref[...]
    xo_ref[...] = _layer_norm(alpha * x_ref[...] + f, lng_ref[...], lnb_ref[...], n_axes=2)


def _combine(y3, pos, g0, g1, x3, lng3, lnb3, alpha, tt):
    n, s, _ = x3.shape
    return pl.pallas_call(
        functools.partial(_combine_kernel, alpha=alpha),
        out_shape=jax.ShapeDtypeStruct(x3.shape, F32),
        grid=(n // tt,),
        in_specs=[pl.BlockSpec((1, TOP_K, tt), lambda i: (i, 0, 0), memory_space=pltpu.SMEM),
                  pl.BlockSpec(memory_space=pl.ANY),
                  pl.BlockSpec((tt, 1, LANES), lambda i: (i, 0, 0)),
                  pl.BlockSpec((tt, 1, LANES), lambda i: (i, 0, 0)),
                  pl.BlockSpec((tt, s, LANES), lambda i: (i, 0, 0)),
                  pl.BlockSpec((1, s, LANES), lambda i: (0, 0, 0)),
                  pl.BlockSpec((1, s, LANES), lambda i: (0, 0, 0))],
        out_specs=pl.BlockSpec((tt, s, LANES), lambda i: (i, 0, 0)),
        scratch_shapes=[pltpu.VMEM((tt, s, LANES), F32),
                        pltpu.VMEM((tt, s, LANES), F32),
                        pltpu.SemaphoreType.DMA],
        compiler_params=_cparams(("arbitrary",), 32),
        name="moe_combine",
    )(pos, y3, g0, g1, x3, lng3, lnb3)


def _moe(xb, x, wr, w1, w3, w2, lng, lnb, alpha, tm, tf, tr, tt):
    n, d = x.shape
    s = d // LANES
    ints, flts, cnt = _router(x, wr, tr)
    e0, e1, r0, r1 = ints[:, 0], ints[:, 1], ints[:, 2], ints[:, 3]
    counts = cnt[0, :N_EXPERTS].astype(I32)

    tiles = (counts + tm - 1) // tm
    tile_end = jnp.cumsum(tiles)
    offs = (tile_end - tiles) * tm
    n_tiles = (TOP_K * n) // tm + N_EXPERTS
    n_valid = tile_end[-1:]
    tile_expert = jnp.minimum(jnp.searchsorted(tile_end, jnp.arange(n_tiles, dtype=I32), side="right"),
                              N_EXPERTS - 1).astype(I32)
    tile_expert = jnp.where(jnp.arange(n_tiles) < n_valid[0], tile_expert,
                            tile_expert[jnp.maximum(n_valid[0] - 1, 0)])
    pos = jnp.stack([offs[e0] + r0, offs[e1] + r1], axis=0)
    pos = pos.reshape(TOP_K, n // tt, tt).transpose(1, 0, 2)

    xs3 = _dispatch(xb.reshape(n, s, LANES), pos, n_tiles * tm, tt)
    y = _moe_ffn(xs3.reshape(n_tiles * tm, d), tile_expert, n_valid.astype(I32), w1, w3, w2, tm, tf)
    g0 = jnp.broadcast_to(flts[:, 0][:, None, None], (n, 1, LANES))
    g1 = jnp.broadcast_to(flts[:, 1][:, None, None], (n, 1, LANES))
    out3 = _combine(y.reshape(n_tiles * tm, s, LANES), pos, g0, g1, x.reshape(n, s, LANES),
                    lng.reshape(1, s, LANES), lnb.reshape(1, s, LANES), alpha, tt)
    return out3.reshape(n, d)


def _rope_tables(t_max):
    half = ATT_HEAD_DIM // 2
    pos = np.arange(t_max)
    pos_r = (pos // GRID_W).astype(np.float32)
    pos_c = (pos % GRID_W).astype(np.float32)
    inv = jnp.asarray(ROPE_THETA, F32) ** (-jnp.arange(0, half, 2, dtype=F32) / half)
    ang_r = jnp.asarray(pos_r)[:, None] * inv
    ang_c = jnp.asarray(pos_c)[:, None] * inv
    cr, sr, cc, sc = jnp.cos(ang_r), jnp.sin(ang_r), jnp.cos(ang_c), jnp.sin(ang_c)
    zero = jnp.zeros_like(sr)
    cos = jnp.concatenate([cr, cr, cc, cc], axis=1)
    sin_up = jnp.concatenate([-sr, zero, -sc, zero], axis=1)
    sin_dn = jnp.concatenate([zero, sr, zero, sc], axis=1)
    return cos, sin_up, sin_dn


def _split_w_in(w_in_l):
    sizes = (ATT_WIDTH, KV_WIDTH, KV_WIDTH, ML_WIDTH, ML_WIDTH, ML_WIDTH, ML_WIDTH, N_ML_GATES)
    o = np.cumsum((0,) + sizes)
    d = w_in_l.shape[0]
    w_gate = jnp.concatenate([w_in_l[:, o[7]:o[8]], jnp.zeros((d, LANES - N_ML_GATES), w_in_l.dtype)], axis=1)
    w_att = jnp.concatenate([w_in_l[:, o[0]:o[3]], w_gate], axis=1).astype(BF16)
    w_rest = jnp.concatenate([w_in_l[:, o[3]:o[7]], w_in_l[:, o[8]:]], axis=1).astype(BF16)
    return w_att, w_rest


def kernel(x_prompt, x_sample, w_in, b_gates, q_gain, k_gain, w_att_br, w_ml_br, w_o, ln1_g, ln1_b,
           w1_d, w3_d, w2_d, router, e_w1, e_w3, e_w2, ln2_g, ln2_b):
    depth = w_in.shape[0]
    d = x_prompt.shape[-1]
    seqs = ((x_prompt.shape[0], x_prompt.shape[1]), (x_sample.shape[0], x_sample.shape[1]))
    n0 = seqs[0][0] * seqs[0][1]
    alpha = float((2 * depth) ** 0.25)

    x = jnp.concatenate([x_prompt.reshape(-1, d), x_sample.reshape(-1, d)], axis=0)
    xb = x.astype(BF16)
    cos, sin_up, sin_dn = _rope_tables(max(seqs[0][1], seqs[1][1]))

    for l in range(depth):
        w_att, w_rest = _split_w_in(w_in[l])
        bias = jnp.concatenate([b_gates[l], jnp.zeros((LANES - N_ML_GATES,), F32)])[None, :]
        gain = jnp.concatenate([jnp.broadcast_to(q_gain[l], (ATT_HEADS, ATT_HEAD_DIM)),
                                jnp.broadcast_to(k_gain[l], (ATT_KV_HEADS, ATT_HEAD_DIM)),
                                jnp.zeros((16 - ATT_HEADS - ATT_KV_HEADS, ATT_HEAD_DIM), F32)], axis=0)

        qkv, gates = _att_proj(xb, w_att, bias, gain, cos, sin_up, sin_dn, seqs, tm=512)
        rest = _matmul(xb, w_rest, BF16, tm=1024, tn=1024)

        a_out = jnp.concatenate(
            [_attention(qkv, 0, seqs[0][0], seqs[0][1], bq=256, tk=512),
             _attention(qkv, n0, seqs[1][0], seqs[1][1], bq=256, tk=512)], axis=0)
        h = _mlstm(rest, gates, gates[:, :N_ML_GATES].T, seqs)

        x, xb = _merge(a_out, h, rest, x, w_att_br[l].astype(BF16), w_ml_br[l].astype(BF16),
                       w_o[l].astype(BF16), ln1_g[l][None, :], ln1_b[l][None, :], alpha, tm=256)

        j = l // 2
        if l % 2 == 0:
            x, xb = _ffn(xb, x, w1_d[j].astype(BF16), w3_d[j].astype(BF16), w2_d[j].astype(BF16),
                         ln2_g[l][None, :], ln2_b[l][None, :], alpha, tm=512, tf=512)
        else:
            wr = jnp.concatenate([router[j], jnp.zeros((d, LANES - N_EXPERTS), F32)], axis=1)
            x = _moe(xb, x, wr, e_w1[j].astype(BF16), e_w3[j].astype(BF16), e_w2[j].astype(BF16),
                     ln2_g[l], ln2_b[l], alpha, tm=512, tf=512, tr=512, tt=256)
            xb = x.astype(BF16)

    return (x[:n0].reshape(x_prompt.shape), x[n0:].reshape(x_sample.shape))
```

```python
import functools
import math

import numpy as np
import jax
import jax.numpy as jnp
from jax import lax
from jax.experimental import pallas as pl
from jax.experimental.pallas import tpu as pltpu

F32 = jnp.float32
BF16 = jnp.bfloat16
I32 = jnp.int32

GRID_W = 64
ATT_HEADS = 8
ATT_KV_HEADS = 2
ATT_HEAD_DIM = 128
ATT_GROUP = ATT_HEADS // ATT_KV_HEADS
ROPE_THETA = 10000.0
QK_EPS = 1e-6
ML_HEADS = 4
ML_HEAD_DIM = 256
ML_CHUNK = 128
ATT_WIDTH = ATT_HEADS * ATT_HEAD_DIM
KV_WIDTH = ATT_KV_HEADS * ATT_HEAD_DIM
ML_WIDTH = ML_HEADS * ML_HEAD_DIM
N_ML_GATES = 4 * ML_HEADS
N_EXPERTS = 8
TOP_K = 2
LN_EPS = 1e-5

LANES = 128
V7X_VMEM_BYTES = 64 * 2 ** 20
MIB = 2 ** 20


def _cparams(semantics, vmem_mib):
    assert vmem_mib * MIB < V7X_VMEM_BYTES
    return pltpu.CompilerParams(dimension_semantics=semantics, vmem_limit_bytes=vmem_mib * MIB)


def _sigmoid(x):
    return 1.0 / (1.0 + jnp.exp(-x))


def _log_sigmoid(x):
    return jnp.minimum(x, 0.0) - jnp.log1p(jnp.exp(-jnp.abs(x)))


def _mean_last(z, n_axes):
    count = 1
    for ax in range(z.ndim - 1, z.ndim - 1 - n_axes, -1):
        count *= z.shape[ax]
        z = jnp.sum(z, axis=ax, keepdims=True)
    return z * (1.0 / count)


def _layer_norm(z, g, b, n_axes=1):
    mu = _mean_last(z, n_axes)
    zc = z - mu
    var = _mean_last(zc * zc, n_axes)
    return zc * lax.rsqrt(var + LN_EPS) * g + b


def _att_proj_kernel(x_ref, w_ref, bias_ref, gain_ref, cos_ref, sa_ref, sb_ref,
                     qkv_ref, gates_ref, acc_ref, *, n_heads, n_norm_heads):
    acc_ref[...] = jnp.dot(x_ref[...], w_ref[...], preferred_element_type=F32)
    cos = cos_ref[...]
    sa = sa_ref[...]
    sb = sb_ref[...]
    ones = jnp.ones((acc_ref.shape[0], LANES), qkv_ref.dtype)
    for h in range(n_heads):
        a = acc_ref[:, h * LANES:(h + 1) * LANES]
        if h < n_norm_heads:
            a = a * lax.rsqrt(jnp.mean(a * a, axis=-1, keepdims=True) + QK_EPS) * gain_ref[h:h + 1, :]
            a = a * cos + pltpu.roll(a, 96, 1) * sa + pltpu.roll(a, 32, 1) * sb
            qkv_ref[:, h * LANES:(h + 1) * LANES] = a.astype(qkv_ref.dtype)
        else:
            c = n_norm_heads + 2 * (h - n_norm_heads)
            qkv_ref[:, c * LANES:(c + 1) * LANES] = a.astype(qkv_ref.dtype)
            qkv_ref[:, (c + 1) * LANES:(c + 2) * LANES] = ones
    gates_ref[...] = acc_ref[:, n_heads * LANES:] + bias_ref[...]


def _att_proj(xb, w, bias, gain, cos, sa, sb, seqs, tm):
    n, d = xb.shape
    n_heads = (ATT_WIDTH + 2 * KV_WIDTH) // LANES
    ncol = n_heads * LANES + LANES
    assert w.shape == (d, ncol)
    n_out = (n_heads + ATT_KV_HEADS) * LANES

    (b0, t0), (b1, t1) = seqs
    assert t0 % tm == 0 and t1 % tm == 0
    nb0 = b0 * t0 // tm

    def pos_map(i):
        return (jnp.where(i < nb0, i % (t0 // tm), (i - nb0) % (t1 // tm)), 0)

    return pl.pallas_call(
        functools.partial(_att_proj_kernel, n_heads=n_heads, n_norm_heads=ATT_HEADS + ATT_KV_HEADS),
        out_shape=(jax.ShapeDtypeStruct((n, n_out), BF16),
                   jax.ShapeDtypeStruct((n, LANES), F32)),
        grid=(n // tm,),
        in_specs=[pl.BlockSpec((tm, d), lambda i: (i, 0)),
                  pl.BlockSpec((d, ncol), lambda i: (0, 0)),
                  pl.BlockSpec((1, LANES), lambda i: (0, 0)),
                  pl.BlockSpec((16, LANES), lambda i: (0, 0)),
                  pl.BlockSpec((tm, LANES), pos_map),
                  pl.BlockSpec((tm, LANES), pos_map),
                  pl.BlockSpec((tm, LANES), pos_map)],
        out_specs=(pl.BlockSpec((tm, n_out), lambda i: (i, 0)),
                   pl.BlockSpec((tm, LANES), lambda i: (i, 0))),
        scratch_shapes=[pltpu.VMEM((tm, ncol), F32)],
        compiler_params=_cparams(("parallel",), 48),
        name="att_proj",
    )(xb, w, bias, gain, cos, sa, sb)


def _matmul_kernel(x_ref, w_ref, o_ref):
    o_ref[...] = jnp.dot(x_ref[...], w_ref[...], preferred_element_type=F32).astype(o_ref.dtype)


def _matmul(xb, w, out_dtype, tm, tn):
    n, d = xb.shape
    ncol = w.shape[1]
    assert n % tm == 0 and ncol % tn == 0
    return pl.pallas_call(
        _matmul_kernel,
        out_shape=jax.ShapeDtypeStruct((n, ncol), out_dtype),
        grid=(ncol // tn, n // tm),
        in_specs=[pl.BlockSpec((tm, d), lambda j, i: (i, 0)),
                  pl.BlockSpec((d, tn), lambda j, i: (0, j))],
        out_specs=pl.BlockSpec((tm, tn), lambda j, i: (i, j)),
        compiler_params=_cparams(("parallel", "parallel"), 48),
        name="matmul",
    )(xb, w)


def _attn_kernel(q_ref, k_ref, v_ref, o_ref, m_ref, acc_ref, *, tk, scale):
    bq = q_ref.shape[0]
    t = k_ref.shape[0]
    c = scale * math.log2(math.e)
    m_ref[...] = jnp.full(m_ref.shape, -jnp.inf, F32)
    acc_ref[...] = jnp.zeros(acc_ref.shape, F32)

    def body(ci, carry):
        start = pl.multiple_of(ci * tk, tk)
        k = k_ref[pl.ds(start, tk), :]
        v = v_ref[pl.ds(start, tk), :]
        for h in range(ATT_GROUP):
            r = slice(h * bq, (h + 1) * bq)
            s = lax.dot_general(q_ref[:, h * LANES:(h + 1) * LANES], k, (((1,), (1,)), ((), ())),
                                preferred_element_type=F32)
            m_prev = m_ref[r, :]
            m_new = jnp.maximum(m_prev, jnp.max(s, axis=1, keepdims=True))
            p = jnp.exp2((s - jnp.tile(m_new, (1, tk // LANES))) * c)
            alpha = jnp.exp2((m_prev - m_new) * c)
            pv = jnp.dot(p.astype(BF16), v, preferred_element_type=F32)
            acc_ref[r, :] = jnp.tile(alpha, (1, 2)) * acc_ref[r, :] + pv
            m_ref[r, :] = m_new
        return carry

    lax.fori_loop(0, t // tk, body, 0)
    for h in range(ATT_GROUP):
        r = slice(h * bq, (h + 1) * bq)
        o_ref[:, h * LANES:(h + 1) * LANES] = (acc_ref[r, :LANES] / acc_ref[r, LANES:]).astype(o_ref.dtype)


def _attention(qkv, tok0, nseq, t, bq, tk):
    assert t % bq == 0 and t % tk == 0 and tok0 % t == 0
    qb0 = tok0 // bq
    sb0 = tok0 // t
    nq = t // bq
    gw = ATT_GROUP * LANES
    v_blk0 = (ATT_HEADS + ATT_KV_HEADS) // 2
    return pl.pallas_call(
        functools.partial(_attn_kernel, tk=tk, scale=ATT_HEAD_DIM ** -0.5),
        out_shape=jax.ShapeDtypeStruct((nseq * t, ATT_WIDTH), BF16),
        grid=(nseq, ATT_KV_HEADS, nq),
        in_specs=[pl.BlockSpec((bq, gw), lambda b, g, i: (qb0 + b * nq + i, g)),
                  pl.BlockSpec((t, LANES), lambda b, g, i: (sb0 + b, ATT_HEADS + g)),
                  pl.BlockSpec((t, 2 * LANES), lambda b, g, i: (sb0 + b, v_blk0 + g))],
        out_specs=pl.BlockSpec((bq, gw), lambda b, g, i: (b * nq + i, g)),
        scratch_shapes=[pltpu.VMEM((ATT_GROUP * bq, LANES), F32),
                        pltpu.VMEM((ATT_GROUP * bq, 2 * LANES), F32)],
        compiler_params=_cparams(("parallel", "parallel", "parallel"), 40),
        name="attention",
    )(qkv, qkv, qkv)


def _mlstm_chain(d, hd, q, k, kt, v, gt, h_ref, c_ref, n_ref, m_ref):
    L, dk = q.shape
    ci = d * ML_HEADS + hd
    cf = 2 * ML_HEADS + ci
    ig_row = gt[ci:ci + 1, :]
    logf_row = _log_sigmoid(gt[cf:cf + 1, :])

    row = lax.broadcasted_iota(I32, (L, L), 0)
    col = lax.broadcasted_iota(I32, (L, L), 1)
    mask = (col <= row) if d == 0 else (col >= row)
    b_col = jnp.sum(jnp.where(mask, logf_row, 0.0), axis=1, keepdims=True)
    b_row = jnp.sum(jnp.where(row == col, b_col, 0.0), axis=0, keepdims=True)
    gtot = jnp.sum(logf_row, axis=1, keepdims=True)

    m_prev = m_ref[...]
    scale = dk ** -0.5
    assert math.log2(dk) % 2 == 0

    a_row = gtot - b_row + ig_row
    m_loc = jnp.max(a_row, axis=1, keepdims=True)
    w_row = jnp.exp(a_row - m_loc) * scale
    c_loc = jnp.dot((kt.astype(F32) * w_row).astype(BF16), v, preferred_element_type=F32)
    n_loc = jnp.dot(jnp.broadcast_to(w_row, (8, L)).astype(BF16), k, preferred_element_type=F32)[0:1, :]

    dmat = jnp.where(mask, b_col - b_row + ig_row, -jnp.inf)
    inter = b_col + m_prev
    m_j = jnp.maximum(jnp.max(dmat, axis=1, keepdims=True), inter)
    qk = jnp.dot(q, kt, preferred_element_type=F32)
    s = qk * (jnp.exp(dmat - m_j) * scale)
    s_int = jnp.exp(inter - m_j)
    num = (jnp.dot(s.astype(BF16), v, preferred_element_type=F32)
           + s_int * jnp.dot(q, c_ref[...].astype(BF16), preferred_element_type=F32))
    den = (jnp.sum(s, axis=1, keepdims=True)
           + s_int * jnp.sum(q.astype(F32) * n_ref[...], axis=1, keepdims=True))
    h_ref[...] = num / jnp.maximum(jnp.abs(den), jnp.exp(-m_j))

    m_new = jnp.maximum(gtot + m_prev, m_loc)
    s_old = jnp.exp(gtot + m_prev - m_new)
    s_new = jnp.exp(m_loc - m_new)
    c_ref[...] = s_old * c_ref[...] + s_new * c_loc
    n_ref[...] = s_old * n_ref[...] + s_new * n_loc
    m_ref[...] = m_new


def _mlstm_kernel(first_ref, qf_ref, kf_ref, vf_ref, ktf_ref, gtf_ref, qb_ref, kb_ref, vb_ref, ktb_ref,
                  gtb_ref, hf_ref, hb_ref, c_ref, n_ref, m_ref):
    i = pl.program_id(0)
    dirs = ((qf_ref, kf_ref, vf_ref, ktf_ref, gtf_ref, hf_ref),
            (qb_ref, kb_ref, vb_ref, ktb_ref, gtb_ref, hb_ref))
    for d in range(2):
        @pl.when(first_ref[d, i] == 1)
        def _():
            for hd in range(ML_HEADS):
                s = d * ML_HEADS + hd
                c_ref[s] = jnp.zeros(c_ref.shape[1:], F32)
                n_ref[s] = jnp.zeros(n_ref.shape[1:], F32)
                m_ref[s] = jnp.zeros(m_ref.shape[1:], F32)

    for d, (q_ref, k_ref, v_ref, kt_ref, gt_ref, h_ref) in enumerate(dirs):
        gt = gt_ref[...]
        for hd in range(ML_HEADS):
            s = d * ML_HEADS + hd
            cols = slice(hd * ML_HEAD_DIM, (hd + 1) * ML_HEAD_DIM)
            _mlstm_chain(d, hd, q_ref[:, cols], k_ref[:, cols], kt_ref[cols, :], v_ref[:, cols], gt,
                         h_ref.at[:, cols], c_ref.at[s], n_ref.at[s], m_ref.at[s])


def _mlstm(rest, k_t, gates_t, seqs):
    n = rest.shape[0]
    L = ML_CHUNK
    nc = n // L
    first_f = np.zeros((nc,), np.int32)
    last_f = np.zeros((nc,), np.int32)
    tok = 0
    for (b, t) in seqs:
        for _ in range(b):
            first_f[tok // L] = 1
            last_f[(tok + t) // L - 1] = 1
            tok += t
    first = np.stack([first_f, last_f[::-1]]).astype(np.int32)
    W = ML_WIDTH
    nstate = 2 * ML_HEADS

    def fwd(c):
        return lambda i, fst: (i, c)

    def bwd(c):
        return lambda i, fst: (nc - 1 - i, c)

    def specs(m, mt):
        return [pl.BlockSpec((L, W), m(0)), pl.BlockSpec((L, W), m(1)), pl.BlockSpec((L, W), m(2)),
                pl.BlockSpec((W, L), mt), pl.BlockSpec((N_ML_GATES, L), mt)]

    grid_spec = pltpu.PrefetchScalarGridSpec(
        num_scalar_prefetch=1,
        grid=(nc,),
        in_specs=(specs(fwd, lambda i, fst: (0, i)) + specs(bwd, lambda i, fst: (0, nc - 1 - i))),
        out_specs=(pl.BlockSpec((L, W), fwd(0)), pl.BlockSpec((L, W), bwd(0))),
        scratch_shapes=[pltpu.VMEM((nstate, ML_HEAD_DIM, ML_HEAD_DIM), F32),
                        pltpu.VMEM((nstate, 1, ML_HEAD_DIM), F32),
                        pltpu.VMEM((nstate, 1, 1), F32)],
    )
    return pl.pallas_call(
        _mlstm_kernel,
        out_shape=(jax.ShapeDtypeStruct((n, W), F32), jax.ShapeDtypeStruct((n, W), F32)),
        grid_spec=grid_spec,
        compiler_params=_cparams(("arbitrary",), 32),
        name="mlstm",
    )(jnp.asarray(first), rest, rest, rest, k_t, gates_t, rest, rest, rest, k_t, gates_t)


def _merge_kernel(a_ref, hf_ref, hb_ref, mo_ref, ga_ref, gm_ref, x_ref, wa_ref, wm_ref, wo_ref,
                  lng_ref, lnb_ref, xo_ref, xob_ref, *, alpha):
    m_out = (_sigmoid(mo_ref[...].astype(F32)) * (hf_ref[...] + hb_ref[...])).astype(BF16)
    pa = jnp.dot(a_ref[...], wa_ref[...], preferred_element_type=F32)
    pm = jnp.dot(m_out, wm_ref[...], preferred_element_type=F32)
    merged = _sigmoid(ga_ref[...].astype(F32)) * pa + _sigmoid(gm_ref[...].astype(F32)) * pm
    y = jnp.dot(merged.astype(BF16), wo_ref[...], preferred_element_type=F32)
    z = _layer_norm(alpha * x_ref[...] + y, lng_ref[...], lnb_ref[...])
    xo_ref[...] = z
    xob_ref[...] = z.astype(BF16)


def _merge(a_out, hf, hb, rest, x, wa, wm, wo, lng, lnb, alpha, tm):
    n, d = x.shape
    assert n % tm == 0 and d == 2 * ML_WIDTH and ATT_WIDTH == ML_WIDTH
    const = dict(pipeline_mode=pl.Buffered(1))
    return pl.pallas_call(
        functools.partial(_merge_kernel, alpha=alpha),
        out_shape=(jax.ShapeDtypeStruct((n, d), F32), jax.ShapeDtypeStruct((n, d), BF16)),
        grid=(n // tm,),
        in_specs=[pl.BlockSpec((tm, ATT_WIDTH), lambda i: (i, 0)),
                  pl.BlockSpec((tm, ML_WIDTH), lambda i: (i, 0)),
                  pl.BlockSpec((tm, ML_WIDTH), lambda i: (i, 0)),
                  pl.BlockSpec((tm, ML_WIDTH), lambda i: (i, 3)),
                  pl.BlockSpec((tm, d), lambda i: (i, 2)),
                  pl.BlockSpec((tm, d), lambda i: (i, 3)),
                  pl.BlockSpec((tm, d), lambda i: (i, 0)),
                  pl.BlockSpec((ATT_WIDTH, d), lambda i: (0, 0), **const),
                  pl.BlockSpec((ML_WIDTH, d), lambda i: (0, 0), **const),
                  pl.BlockSpec((d, d), lambda i: (0, 0), **const),
                  pl.BlockSpec((1, d), lambda i: (0, 0)),
                  pl.BlockSpec((1, d), lambda i: (0, 0))],
        out_specs=(pl.BlockSpec((tm, d), lambda i: (i, 0)),
                   pl.BlockSpec((tm, d), lambda i: (i, 0))),
        compiler_params=_cparams(("parallel",), 56),
        name="merge",
    )(a_out, hf, hb, rest, rest, rest, x, wa, wm, wo, lng, lnb)


def _ffn_kernel(xb_ref, w1_ref, w3_ref, w2_ref, x_ref, lng_ref, lnb_ref, xo_ref, xob_ref, acc_ref,
                *, alpha):
    j = pl.program_id(1)

    @pl.when(j == 0)
    def _():
        acc_ref[...] = jnp.zeros(acc_ref.shape, F32)

    xb = xb_ref[...]
    a = jnp.dot(xb, w1_ref[...], preferred_element_type=F32)
    b = jnp.dot(xb, w3_ref[...], preferred_element_type=F32)
    hid = (a * _sigmoid(a) * b).astype(BF16)
    acc_ref[...] += jnp.dot(hid, w2_ref[...], preferred_element_type=F32)

    @pl.when(j == pl.num_programs(1) - 1)
    def _():
        z = _layer_norm(alpha * x_ref[...] + acc_ref[...], lng_ref[...], lnb_ref[...])
        xo_ref[...] = z
        xob_ref[...] = z.astype(BF16)


def _ffn(xb, x, w1, w3, w2, lng, lnb, alpha, tm, tf):
    n, d = x.shape
    f = w1.shape[1]
    assert n % tm == 0 and f % tf == 0
    return pl.pallas_call(
        functools.partial(_ffn_kernel, alpha=alpha),
        out_shape=(jax.ShapeDtypeStruct((n, d), F32), jax.ShapeDtypeStruct((n, d), BF16)),
        grid=(n // tm, f // tf),
        in_specs=[pl.BlockSpec((tm, d), lambda i, j: (i, 0)),
                  pl.BlockSpec((d, tf), lambda i, j: (0, j)),
                  pl.BlockSpec((d, tf), lambda i, j: (0, j)),
                  pl.BlockSpec((tf, d), lambda i, j: (j, 0)),
                  pl.BlockSpec((tm, d), lambda i, j: (i, 0)),
                  pl.BlockSpec((1, d), lambda i, j: (0, 0)),
                  pl.BlockSpec((1, d), lambda i, j: (0, 0))],
        out_specs=(pl.BlockSpec((tm, d), lambda i, j: (i, 0)),
                   pl.BlockSpec((tm, d), lambda i, j: (i, 0))),
        scratch_shapes=[pltpu.VMEM((tm, d), F32)],
        compiler_params=_cparams(("parallel", "arbitrary"), 56),
        name="ffn_dense",
    )(xb, w1, w3, w2, x, lng, lnb)


def _router_kernel(x_ref, wr_ref, tri_ref, ints_ref, flts_ref, cnt_ref, base_ref):
    i = pl.program_id(0)

    @pl.when(i == 0)
    def _():
        base_ref[...] = jnp.zeros(base_ref.shape, F32)

    logits = jnp.dot(x_ref[...], wr_ref[...], preferred_element_type=F32,
                     precision=lax.Precision.HIGHEST)
    lane = lax.broadcasted_iota(I32, logits.shape, 1)
    lanef = lane.astype(F32)
    logits = jnp.where(lane < N_EXPERTS, logits, -jnp.inf)
    v0 = jnp.max(logits, axis=1, keepdims=True)
    i0 = jnp.min(jnp.where(logits == v0, lanef, float(LANES)), axis=1, keepdims=True)
    rest = jnp.where(lanef == i0, -jnp.inf, logits)
    v1 = jnp.max(rest, axis=1, keepdims=True)
    i1 = jnp.min(jnp.where(rest == v1, lanef, float(LANES)), axis=1, keepdims=True)
    e1 = jnp.exp(v1 - v0)
    g0 = 1.0 / (1.0 + e1)
    g1 = e1 / (1.0 + e1)

    sel0 = lanef == i0
    sel1 = lanef == i1
    onehot = jnp.where(sel0 | sel1, 1.0, 0.0)
    before = jnp.dot(tri_ref[...], onehot.astype(BF16), preferred_element_type=F32) + base_ref[...]
    r0 = jnp.sum(jnp.where(sel0, before, 0.0), axis=1, keepdims=True)
    r1 = jnp.sum(jnp.where(sel1, before, 0.0), axis=1, keepdims=True)
    base_ref[...] += jnp.sum(onehot, axis=0, keepdims=True)
    cnt_ref[...] = jnp.broadcast_to(base_ref[...], cnt_ref.shape)

    ints = jnp.where(lane == 0, i0, jnp.where(lane == 1, i1, jnp.where(lane == 2, r0,
                     jnp.where(lane == 3, r1, 0.0))))
    ints_ref[...] = ints.astype(I32)
    flts_ref[...] = jnp.where(lane == 0, g0, jnp.where(lane == 1, g1, 0.0))


def _router(x, wr, tr):
    n, d = x.shape
    assert n % tr == 0
    tri = jnp.asarray(np.tril(np.ones((tr, tr), np.float32), -1), BF16)
    return pl.pallas_call(
        _router_kernel,
        out_shape=(jax.ShapeDtypeStruct((n, LANES), I32),
                   jax.ShapeDtypeStruct((n, LANES), F32),
                   jax.ShapeDtypeStruct((8, LANES), F32)),
        grid=(n // tr,),
        in_specs=[pl.BlockSpec((tr, d), lambda i: (i, 0)),
                  pl.BlockSpec((d, LANES), lambda i: (0, 0)),
                  pl.BlockSpec((tr, tr), lambda i: (0, 0))],
        out_specs=(pl.BlockSpec((tr, LANES), lambda i: (i, 0)),
                   pl.BlockSpec((tr, LANES), lambda i: (i, 0)),
                   pl.BlockSpec((8, LANES), lambda i: (0, 0))),
        scratch_shapes=[pltpu.VMEM((1, LANES), F32)],
        compiler_params=_cparams(("arbitrary",), 32),
        name="moe_router",
    )(x, wr, tri)


def _dispatch_kernel(pos_ref, x_ref, xs_in_ref, xs_ref, sem):
    del xs_in_ref
    tt = x_ref.shape[0]

    def copy(t, k):
        return pltpu.make_async_copy(x_ref.at[t], xs_ref.at[pos_ref[0, k, t]], sem)

    def issue(t, carry):
        copy(t, 0).start()
        copy(t, 1).start()
        return carry

    def drain(t, carry):
        copy(t, 0).wait()
        copy(t, 1).wait()
        return carry

    lax.fori_loop(0, tt, issue, 0)
    lax.fori_loop(0, tt, drain, 0)


def _dispatch(x3, pos, n_rows, tt):
    n, s, _ = x3.shape
    xs0 = jnp.zeros((n_rows, s, LANES), x3.dtype)
    return pl.pallas_call(
        _dispatch_kernel,
        out_shape=jax.ShapeDtypeStruct(xs0.shape, xs0.dtype),
        grid=(n // tt,),
        in_specs=[pl.BlockSpec((1, TOP_K, tt), lambda i: (i, 0, 0), memory_space=pltpu.SMEM),
                  pl.BlockSpec((tt, s, LANES), lambda i: (i, 0, 0)),
                  pl.BlockSpec(memory_space=pl.ANY)],
        out_specs=pl.BlockSpec(memory_space=pl.ANY),
        scratch_shapes=[pltpu.SemaphoreType.DMA],
        input_output_aliases={2: 0},
        compiler_params=_cparams(("arbitrary",), 32),
        name="moe_dispatch",
    )(pos, x3, xs0)


def _moe_ffn_kernel(te_ref, nv_ref, xs_ref, w1_ref, w3_ref, w2_ref, y_ref):
    i = pl.program_id(0)
    j = pl.program_id(1)

    @pl.when(i < nv_ref[0])
    def _():
        @pl.when(j == 0)
        def _():
            y_ref[...] = jnp.zeros(y_ref.shape, F32)

        xb = xs_ref[...]
        a = jnp.dot(xb, w1_ref[...], preferred_element_type=F32)
        b = jnp.dot(xb, w3_ref[...], preferred_element_type=F32)
        hid = (a * _sigmoid(a) * b).astype(BF16)
        y_ref[...] += jnp.dot(hid, w2_ref[...], preferred_element_type=F32)

    @pl.when((i >= nv_ref[0]) & (j == 0))
    def _():
        y_ref[...] = jnp.zeros(y_ref.shape, F32)


def _moe_ffn(xs, tile_expert, n_valid, w1, w3, w2, tm, tf):
    p, d = xs.shape
    f = w1.shape[2]
    assert p % tm == 0 and f % tf == 0
    nj = f // tf

    def row_map(i, j, te, nv):
        return (jnp.minimum(i, nv[0] - 1), 0)

    def jj(i, j, nv):
        return jnp.where(i < nv[0], j, nj - 1)

    grid_spec = pltpu.PrefetchScalarGridSpec(
        num_scalar_prefetch=2,
        grid=(p // tm, nj),
        in_specs=[pl.BlockSpec((tm, d), row_map),
                  pl.BlockSpec((None, d, tf), lambda i, j, te, nv: (te[i], 0, jj(i, j, nv))),
                  pl.BlockSpec((None, d, tf), lambda i, j, te, nv: (te[i], 0, jj(i, j, nv))),
                  pl.BlockSpec((None, tf, d), lambda i, j, te, nv: (te[i], jj(i, j, nv), 0))],
        out_specs=pl.BlockSpec((tm, d), lambda i, j, te, nv: (i, 0)),
    )
    return pl.pallas_call(
        _moe_ffn_kernel,
        out_shape=jax.ShapeDtypeStruct((p, d), F32),
        grid_spec=grid_spec,
        compiler_params=_cparams(("arbitrary", "arbitrary"), 56),
        name="moe_ffn",
    )(tile_expert, n_valid, xs, w1, w3, w2)


def _combine_kernel(pos_ref, y_ref, g0_ref, g1_ref, x_ref, lng_ref, lnb_ref, xo_ref,
                    buf0_ref, buf1_ref, sem, *, alpha):
    tt = x_ref.shape[0]

    def copy(t, k):
        buf = buf0_ref if k == 0 else buf1_ref
        return pltpu.make_async_copy(y_ref.at[pos_ref[0, k, t]], buf.at[t], sem)

    def issue(t, carry):
        copy(t, 0).start()
        copy(t, 1).start()
        return carry

    def drain(t, carry):
        copy(t, 0).wait()
        copy(t, 1).wait()
        return carry

    lax.fori_loop(0, tt, issue, 0)
    lax.fori_loop(0, tt, drain, 0)
    f = g0_ref[...] * buf0_ref[...] + g1_ref[...] * buf1_ref[...]
    xo_ref[...] = _layer_norm(alpha * x_ref[...] + f, lng_ref[...], lnb_ref[...], n_axes=2)


def _combine(y3, pos, g0, g1, x3, lng3, lnb3, alpha, tt):
    n, s, _ = x3.shape
    return pl.pallas_call(
        functools.partial(_combine_kernel, alpha=alpha),
        out_shape=jax.ShapeDtypeStruct(x3.shape, F32),
        grid=(n // tt,),
        in_specs=[pl.BlockSpec((1, TOP_K, tt), lambda i: (i, 0, 0), memory_space=pltpu.SMEM),
                  pl.BlockSpec(memory_space=pl.ANY),
                  pl.BlockSpec((tt, 1, LANES), lambda i: (i, 0, 0)),
                  pl.BlockSpec((tt, 1, LANES), lambda i: (i, 0, 0)),
                  pl.BlockSpec((tt, s, LANES), lambda i: (i, 0, 0)),
                  pl.BlockSpec((1, s, LANES), lambda i: (0, 0, 0)),
                  pl.BlockSpec((1, s, LANES), lambda i: (0, 0, 0))],
        out_specs=pl.BlockSpec((tt, s, LANES), lambda i: (i, 0, 0)),
        scratch_shapes=[pltpu.VMEM((tt, s, LANES), F32),
                        pltpu.VMEM((tt, s, LANES), F32),
                        pltpu.SemaphoreType.DMA],
        compiler_params=_cparams(("arbitrary",), 32),
        name="moe_combine",
    )(pos, y3, g0, g1, x3, lng3, lnb3)


def _moe(xb, x, wr, w1, w3, w2, lng, lnb, alpha, tm, tf, tr, tt):
    n, d = x.shape
    s = d // LANES
    ints, flts, cnt = _router(x, wr, tr)
    e0, e1, r0, r1 = ints[:, 0], ints[:, 1], ints[:, 2], ints[:, 3]
    counts = cnt[0, :N_EXPERTS].astype(I32)

    tiles = (counts + tm - 1) // tm
    tile_end = jnp.cumsum(tiles)
    offs = (tile_end - tiles) * tm
    n_tiles = (TOP_K * n) // tm + N_EXPERTS
    n_valid = tile_end[-1:]
    tile_expert = jnp.sum(jnp.arange(n_tiles, dtype=I32)[:, None] >= tile_end[None, :], axis=1)
    tile_expert = jnp.minimum(tile_expert, N_EXPERTS - 1).astype(I32)
    tile_expert = jnp.where(jnp.arange(n_tiles) < n_valid[0], tile_expert,
                            tile_expert[jnp.maximum(n_valid[0] - 1, 0)])
    pos = jnp.stack([offs[e0] + r0, offs[e1] + r1], axis=0)
    pos = pos.reshape(TOP_K, n // tt, tt).transpose(1, 0, 2)

    xs3 = _dispatch(xb.reshape(n, s, LANES), pos, n_tiles * tm, tt)
    y = _moe_ffn(xs3.reshape(n_tiles * tm, d), tile_expert, n_valid.astype(I32), w1, w3, w2, tm, tf)
    g0 = jnp.broadcast_to(flts[:, 0][:, None, None], (n, 1, LANES))
    g1 = jnp.broadcast_to(flts[:, 1][:, None, None], (n, 1, LANES))
    out3 = _combine(y.reshape(n_tiles * tm, s, LANES), pos, g0, g1, x.reshape(n, s, LANES),
                    lng.reshape(1, s, LANES), lnb.reshape(1, s, LANES), alpha, tt)
    return out3.reshape(n, d)


def _rope_tables(t_max):
    half = ATT_HEAD_DIM // 2
    pos = np.arange(t_max)
    pos_r = (pos // GRID_W).astype(np.float32)
    pos_c = (pos % GRID_W).astype(np.float32)
    inv = jnp.asarray(ROPE_THETA, F32) ** (-jnp.arange(0, half, 2, dtype=F32) / half)
    ang_r = jnp.asarray(pos_r)[:, None] * inv
    ang_c = jnp.asarray(pos_c)[:, None] * inv
    cr, sr, cc, sc = jnp.cos(ang_r), jnp.sin(ang_r), jnp.cos(ang_c), jnp.sin(ang_c)
    zero = jnp.zeros_like(sr)
    cos = jnp.concatenate([cr, cr, cc, cc], axis=1)
    sin_up = jnp.concatenate([-sr, zero, -sc, zero], axis=1)
    sin_dn = jnp.concatenate([zero, sr, zero, sc], axis=1)
    return cos, sin_up, sin_dn


def _split_w_in(w_in_l):
    sizes = (ATT_WIDTH, KV_WIDTH, KV_WIDTH, ML_WIDTH, ML_WIDTH, ML_WIDTH, ML_WIDTH, N_ML_GATES)
    o = np.cumsum((0,) + sizes)
    d = w_in_l.shape[0]
    w_gate = jnp.concatenate([w_in_l[:, o[7]:o[8]], jnp.zeros((d, LANES - N_ML_GATES), w_in_l.dtype)], axis=1)
    w_att = jnp.concatenate([w_in_l[:, o[0]:o[3]], w_gate], axis=1).astype(BF16)
    w_rest = jnp.concatenate([w_in_l[:, o[3]:o[7]], w_in_l[:, o[8]:]], axis=1).astype(BF16)
    return w_att, w_rest


def kernel(x_prompt, x_sample, w_in, b_gates, q_gain, k_gain, w_att_br, w_ml_br, w_o, ln1_g, ln1_b,
           w1_d, w3_d, w2_d, router, e_w1, e_w3, e_w2, ln2_g, ln2_b):
    depth = w_in.shape[0]
    d = x_prompt.shape[-1]
    seqs = ((x_prompt.shape[0], x_prompt.shape[1]), (x_sample.shape[0], x_sample.shape[1]))
    n0 = seqs[0][0] * seqs[0][1]
    alpha = float((2 * depth) ** 0.25)

    x = jnp.concatenate([x_prompt.reshape(-1, d), x_sample.reshape(-1, d)], axis=0)
    xb = x.astype(BF16)
    cos, sin_up, sin_dn = _rope_tables(max(seqs[0][1], seqs[1][1]))

    for l in range(depth):
        w_att, w_rest = _split_w_in(w_in[l])
        bias = jnp.concatenate([b_gates[l], jnp.zeros((LANES - N_ML_GATES,), F32)])[None, :]
        gain = jnp.concatenate([jnp.broadcast_to(q_gain[l], (ATT_HEADS, ATT_HEAD_DIM)),
                                jnp.broadcast_to(k_gain[l], (ATT_KV_HEADS, ATT_HEAD_DIM)),
                                jnp.zeros((16 - ATT_HEADS - ATT_KV_HEADS, ATT_HEAD_DIM), F32)], axis=0)

        qkv, gates = _att_proj(xb, w_att, bias, gain, cos, sin_up, sin_dn, seqs, tm=512)
        rest = _matmul(xb, w_rest, BF16, tm=1024, tn=1024)

        a_out = jnp.concatenate(
            [_attention(qkv, 0, seqs[0][0], seqs[0][1], bq=min(1024, seqs[0][1]), tk=512),
             _attention(qkv, n0, seqs[1][0], seqs[1][1], bq=min(1024, seqs[1][1]), tk=512)], axis=0)
        hf, hb = _mlstm(rest, rest[:, ML_WIDTH:2 * ML_WIDTH].T, gates[:, :N_ML_GATES].T, seqs)

        x, xb = _merge(a_out, hf, hb, rest, x, w_att_br[l].astype(BF16), w_ml_br[l].astype(BF16),
                       w_o[l].astype(BF16), ln1_g[l][None, :], ln1_b[l][None, :], alpha, tm=256)

        j = l // 2
        if l % 2 == 0:
            x, xb = _ffn(xb, x, w1_d[j].astype(BF16), w3_d[j].astype(BF16), w2_d[j].astype(BF16),
                         ln2_g[l][None, :], ln2_b[l][None, :], alpha, tm=512, tf=512)
        else:
            wr = jnp.concatenate([router[j], jnp.zeros((d, LANES - N_EXPERTS), F32)], axis=1)
            x = _moe(xb, x, wr, e_w1[j].astype(BF16), e_w3[j].astype(BF16), e_w2[j].astype(BF16),
                     ln2_g[l], ln2_b[l], alpha, tm=512, tf=512, tr=512, tt=256)
            xb = x.astype(BF16)

    return (x[:n0].reshape(x_prompt.shape), x[n0:].reshape(x_sample.shape))
```

```python
import functools
import math

import numpy as np
import jax
import jax.numpy as jnp
from jax import lax
from jax.experimental import pallas as pl
from jax.experimental.pallas import tpu as pltpu

F32 = jnp.float32
BF16 = jnp.bfloat16
I32 = jnp.int32

GRID_W = 64
ATT_HEADS = 8
ATT_KV_HEADS = 2
ATT_HEAD_DIM = 128
ATT_GROUP = ATT_HEADS // ATT_KV_HEADS
ROPE_THETA = 10000.0
QK_EPS = 1e-6
ML_HEADS = 4
ML_HEAD_DIM = 256
ML_CHUNK = 128
ATT_WIDTH = ATT_HEADS * ATT_HEAD_DIM
KV_WIDTH = ATT_KV_HEADS * ATT_HEAD_DIM
ML_WIDTH = ML_HEADS * ML_HEAD_DIM
N_ML_GATES = 4 * ML_HEADS
N_EXPERTS = 8
TOP_K = 2
LN_EPS = 1e-5

LANES = 128
V7X_VMEM_BYTES = 64 * 2 ** 20
MIB = 2 ** 20

TILES = dict(att_proj_tm=512, proj_tm=1024, proj_tn=1024, attn_bq=1024, attn_tk=512, merge_tm=256,
             ffn_tm=512, ffn_tf=512, moe_tm=512, moe_tf=512, router_tr=512, moe_tt=256)


def _cparams(semantics, vmem_mib):
    assert vmem_mib * MIB < V7X_VMEM_BYTES
    return pltpu.CompilerParams(dimension_semantics=semantics, vmem_limit_bytes=vmem_mib * MIB)


def _sigmoid(x):
    return 1.0 / (1.0 + jnp.exp(-x))


def _log_sigmoid(x):
    return jnp.minimum(x, 0.0) - jnp.log1p(jnp.exp(-jnp.abs(x)))


def _mean_last(z, n_axes):
    count = 1
    for ax in range(z.ndim - 1, z.ndim - 1 - n_axes, -1):
        count *= z.shape[ax]
        z = jnp.sum(z, axis=ax, keepdims=True)
    return z * (1.0 / count)


def _layer_norm(z, g, b, n_axes=1):
    mu = _mean_last(z, n_axes)
    zc = z - mu
    var = _mean_last(zc * zc, n_axes)
    return zc * lax.rsqrt(var + LN_EPS) * g + b


def _att_proj_kernel(x_ref, w_ref, bias_ref, gain_ref, cos_ref, sa_ref, sb_ref,
                     qkv_ref, gates_ref, *, n_heads, n_norm_heads):
    x = x_ref[...]
    cos = cos_ref[...]
    sa = sa_ref[...]
    sb = sb_ref[...]
    ones = jnp.ones((x.shape[0], LANES), qkv_ref.dtype)
    for h in range(n_heads):
        if h % 2 == 0:
            pair = jnp.dot(x, w_ref[:, h * LANES:(h + 2) * LANES], preferred_element_type=F32)
        a = pair[:, (h % 2) * LANES:(h % 2 + 1) * LANES]
        if h < n_norm_heads:
            a = a * lax.rsqrt(jnp.mean(a * a, axis=-1, keepdims=True) + QK_EPS) * gain_ref[h:h + 1, :]
            a = a * cos + pltpu.roll(a, 96, 1) * sa + pltpu.roll(a, 32, 1) * sb
            qkv_ref[:, h * LANES:(h + 1) * LANES] = a.astype(qkv_ref.dtype)
        else:
            c = n_norm_heads + 2 * (h - n_norm_heads)
            qkv_ref[:, c * LANES:(c + 1) * LANES] = a.astype(qkv_ref.dtype)
            qkv_ref[:, (c + 1) * LANES:(c + 2) * LANES] = ones
    gates_ref[...] = (jnp.dot(x, w_ref[:, n_heads * LANES:], preferred_element_type=F32)
                      + bias_ref[...])


def _att_proj(xb, w, bias, gain, cos, sa, sb, seqs, tm):
    n, d = xb.shape
    n_heads = (ATT_WIDTH + 2 * KV_WIDTH) // LANES
    ncol = n_heads * LANES + LANES
    assert w.shape == (d, ncol)
    n_out = (n_heads + ATT_KV_HEADS) * LANES

    (b0, t0), (b1, t1) = seqs
    assert t0 % tm == 0 and t1 % tm == 0
    nb0 = b0 * t0 // tm

    def pos_map(i):
        return (jnp.where(i < nb0, i % (t0 // tm), (i - nb0) % (t1 // tm)), 0)

    return pl.pallas_call(
        functools.partial(_att_proj_kernel, n_heads=n_heads, n_norm_heads=ATT_HEADS + ATT_KV_HEADS),
        out_shape=(jax.ShapeDtypeStruct((n, n_out), BF16),
                   jax.ShapeDtypeStruct((n, LANES), F32)),
        grid=(n // tm,),
        in_specs=[pl.BlockSpec((tm, d), lambda i: (i, 0)),
                  pl.BlockSpec((d, ncol), lambda i: (0, 0)),
                  pl.BlockSpec((1, LANES), lambda i: (0, 0)),
                  pl.BlockSpec((16, LANES), lambda i: (0, 0)),
                  pl.BlockSpec((tm, LANES), pos_map),
                  pl.BlockSpec((tm, LANES), pos_map),
                  pl.BlockSpec((tm, LANES), pos_map)],
        out_specs=(pl.BlockSpec((tm, n_out), lambda i: (i, 0)),
                   pl.BlockSpec((tm, LANES), lambda i: (i, 0))),
        compiler_params=_cparams(("parallel",), 48),
        name="att_proj",
    )(xb, w, bias, gain, cos, sa, sb)


def _matmul_kernel(x_ref, w_ref, o_ref):
    o_ref[...] = jnp.dot(x_ref[...], w_ref[...], preferred_element_type=F32).astype(o_ref.dtype)


def _matmul(xb, w, out_dtype, tm, tn):
    n, d = xb.shape
    ncol = w.shape[1]
    assert n % tm == 0 and ncol % tn == 0
    return pl.pallas_call(
        _matmul_kernel,
        out_shape=jax.ShapeDtypeStruct((n, ncol), out_dtype),
        grid=(ncol // tn, n // tm),
        in_specs=[pl.BlockSpec((tm, d), lambda j, i: (i, 0)),
                  pl.BlockSpec((d, tn), lambda j, i: (0, j))],
        out_specs=pl.BlockSpec((tm, tn), lambda j, i: (i, j)),
        compiler_params=_cparams(("parallel", "parallel"), 48),
        name="matmul",
    )(xb, w)


def _attn_kernel(q_ref, k_ref, v_ref, o_ref, m_ref, acc_ref, *, tk, scale):
    bq = q_ref.shape[0]
    t = k_ref.shape[0]
    c = scale * math.log2(math.e)
    m_ref[...] = jnp.full(m_ref.shape, -jnp.inf, F32)
    acc_ref[...] = jnp.zeros(acc_ref.shape, F32)

    def body(ci, carry):
        start = pl.multiple_of(ci * tk, tk)
        k = k_ref[pl.ds(start, tk), :]
        v = v_ref[pl.ds(start, tk), :]
        for h in range(ATT_GROUP):
            r = slice(h * bq, (h + 1) * bq)
            s = lax.dot_general(q_ref[:, h * LANES:(h + 1) * LANES], k, (((1,), (1,)), ((), ())),
                                preferred_element_type=F32)
            m_prev = m_ref[r, :]
            m_new = jnp.maximum(m_prev, jnp.max(s, axis=1, keepdims=True))
            p = jnp.exp2((s - jnp.tile(m_new, (1, tk // LANES))) * c)
            alpha = jnp.exp2((m_prev - m_new) * c)
            pv = jnp.dot(p.astype(BF16), v, preferred_element_type=F32)
            acc_ref[r, :] = jnp.tile(alpha, (1, 2)) * acc_ref[r, :] + pv
            m_ref[r, :] = m_new
        return carry

    lax.fori_loop(0, t // tk, body, 0)
    for h in range(ATT_GROUP):
        r = slice(h * bq, (h + 1) * bq)
        o_ref[:, h * LANES:(h + 1) * LANES] = (acc_ref[r, :LANES] / acc_ref[r, LANES:]).astype(o_ref.dtype)


def _attention(qkv, tok0, nseq, t, bq, tk):
    assert t % bq == 0 and t % tk == 0 and tok0 % t == 0
    qb0 = tok0 // bq
    sb0 = tok0 // t
    nq = t // bq
    gw = ATT_GROUP * LANES
    v_blk0 = (ATT_HEADS + ATT_KV_HEADS) // 2
    return pl.pallas_call(
        functools.partial(_attn_kernel, tk=tk, scale=ATT_HEAD_DIM ** -0.5),
        out_shape=jax.ShapeDtypeStruct((nseq * t, ATT_WIDTH), BF16),
        grid=(nseq, ATT_KV_HEADS, nq),
        in_specs=[pl.BlockSpec((bq, gw), lambda b, g, i: (qb0 + b * nq + i, g)),
                  pl.BlockSpec((t, LANES), lambda b, g, i: (sb0 + b, ATT_HEADS + g)),
                  pl.BlockSpec((t, 2 * LANES), lambda b, g, i: (sb0 + b, v_blk0 + g))],
        out_specs=pl.BlockSpec((bq, gw), lambda b, g, i: (b * nq + i, g)),
        scratch_shapes=[pltpu.VMEM((ATT_GROUP * bq, LANES), F32),
                        pltpu.VMEM((ATT_GROUP * bq, 2 * LANES), F32)],
        compiler_params=_cparams(("parallel", "parallel", "parallel"), 40),
        name="attention",
    )(qkv, qkv, qkv)


def _mlstm_chain(d, hd, q, k, kt, v, gt, h_ref, c_ref, n_ref, m_ref):
    L, dk = q.shape
    ci = d * ML_HEADS + hd
    cf = 2 * ML_HEADS + ci
    ig_row = gt[ci:ci + 1, :]
    logf_row = _log_sigmoid(gt[cf:cf + 1, :])

    row = lax.broadcasted_iota(I32, (L, L), 0)
    col = lax.broadcasted_iota(I32, (L, L), 1)
    mask = (col <= row) if d == 0 else (col >= row)
    b_col = jnp.sum(jnp.where(mask, logf_row, 0.0), axis=1, keepdims=True)
    b_row = jnp.sum(jnp.where(row == col, b_col, 0.0), axis=0, keepdims=True)
    gtot = jnp.sum(logf_row, axis=1, keepdims=True)

    m_prev = m_ref[...]
    scale = dk ** -0.5
    assert math.log2(dk) % 2 == 0

    a_row = gtot - b_row + ig_row
    m_loc = jnp.max(a_row, axis=1, keepdims=True)
    w_row = jnp.exp(a_row - m_loc) * scale
    c_loc = jnp.dot((kt.astype(F32) * w_row).astype(BF16), v, preferred_element_type=F32)
    n_loc = jnp.dot(jnp.broadcast_to(w_row, (8, L)).astype(BF16), k, preferred_element_type=F32)[0:1, :]

    dmat = jnp.where(mask, b_col - b_row + ig_row, -jnp.inf)
    inter = b_col + m_prev
    m_j = jnp.maximum(jnp.max(dmat, axis=1, keepdims=True), inter)
    qk = jnp.dot(q, kt, preferred_element_type=F32)
    s = qk * (jnp.exp(dmat - m_j) * scale)
    s_int = jnp.exp(inter - m_j)
    num = (jnp.dot(s.astype(BF16), v, preferred_element_type=F32)
           + s_int * jnp.dot(q, c_ref[...].astype(BF16), preferred_element_type=F32))
    den = (jnp.sum(s, axis=1, keepdims=True)
           + s_int * jnp.sum(q.astype(F32) * n_ref[...], axis=1, keepdims=True))
    h_ref[...] = num / jnp.maximum(jnp.abs(den), jnp.exp(-m_j))

    m_new = jnp.maximum(gtot + m_prev, m_loc)
    s_old = jnp.exp(gtot + m_prev - m_new)
    s_new = jnp.exp(m_loc - m_new)
    c_ref[...] = s_old * c_ref[...] + s_new * c_loc
    n_ref[...] = s_old * n_ref[...] + s_new * n_loc
    m_ref[...] = m_new


def _mlstm_kernel(first_ref, qf_ref, kf_ref, vf_ref, ktf_ref, gtf_ref, qb_ref, kb_ref, vb_ref, ktb_ref,
                  gtb_ref, hf_ref, hb_ref, c_ref, n_ref, m_ref):
    i = pl.program_id(0)
    dirs = ((qf_ref, kf_ref, vf_ref, ktf_ref, gtf_ref, hf_ref),
            (qb_ref, kb_ref, vb_ref, ktb_ref, gtb_ref, hb_ref))
    for d in range(2):
        @pl.when(first_ref[d, i] == 1)
        def _():
            for hd in range(ML_HEADS):
                s = d * ML_HEADS + hd
                c_ref[s] = jnp.zeros(c_ref.shape[1:], F32)
                n_ref[s] = jnp.zeros(n_ref.shape[1:], F32)
                m_ref[s] = jnp.zeros(m_ref.shape[1:], F32)

    for d, (q_ref, k_ref, v_ref, kt_ref, gt_ref, h_ref) in enumerate(dirs):
        gt = gt_ref[...]
        for hd in range(ML_HEADS):
            s = d * ML_HEADS + hd
            cols = slice(hd * ML_HEAD_DIM, (hd + 1) * ML_HEAD_DIM)
            _mlstm_chain(d, hd, q_ref[:, cols], k_ref[:, cols], kt_ref[cols, :], v_ref[:, cols], gt,
                         h_ref.at[:, cols], c_ref.at[s], n_ref.at[s], m_ref.at[s])


def _mlstm(rest, k_t, gates_t, seqs):
    n = rest.shape[0]
    L = ML_CHUNK
    nc = n // L
    first_f = np.zeros((nc,), np.int32)
    last_f = np.zeros((nc,), np.int32)
    tok = 0
    for (b, t) in seqs:
        for _ in range(b):
            first_f[tok // L] = 1
            last_f[(tok + t) // L - 1] = 1
            tok += t
    first = np.stack([first_f, last_f[::-1]]).astype(np.int32)
    W = ML_WIDTH
    nstate = 2 * ML_HEADS

    def fwd(c):
        return lambda i, fst: (i, c)

    def bwd(c):
        return lambda i, fst: (nc - 1 - i, c)

    def specs(m, mt):
        return [pl.BlockSpec((L, W), m(0)), pl.BlockSpec((L, W), m(1)), pl.BlockSpec((L, W), m(2)),
                pl.BlockSpec((W, L), mt), pl.BlockSpec((N_ML_GATES, L), mt)]

    grid_spec = pltpu.PrefetchScalarGridSpec(
        num_scalar_prefetch=1,
        grid=(nc,),
        in_specs=(specs(fwd, lambda i, fst: (0, i)) + specs(bwd, lambda i, fst: (0, nc - 1 - i))),
        out_specs=(pl.BlockSpec((L, W), fwd(0)), pl.BlockSpec((L, W), bwd(0))),
        scratch_shapes=[pltpu.VMEM((nstate, ML_HEAD_DIM, ML_HEAD_DIM), F32),
                        pltpu.VMEM((nstate, 1, ML_HEAD_DIM), F32),
                        pltpu.VMEM((nstate, 1, 1), F32)],
    )
    return pl.pallas_call(
        _mlstm_kernel,
        out_shape=(jax.ShapeDtypeStruct((n, W), F32), jax.ShapeDtypeStruct((n, W), F32)),
        grid_spec=grid_spec,
        compiler_params=_cparams(("arbitrary",), 32),
        name="mlstm",
    )(jnp.asarray(first), rest, rest, rest, k_t, gates_t, rest, rest, rest, k_t, gates_t)


def _merge_kernel(a_ref, hf_ref, hb_ref, mo_ref, ga_ref, gm_ref, x_ref, wa_ref, wm_ref, wo_ref,
                  lng_ref, lnb_ref, xo_ref, xob_ref, *, alpha):
    m_out = (_sigmoid(mo_ref[...].astype(F32)) * (hf_ref[...] + hb_ref[...])).astype(BF16)
    pa = jnp.dot(a_ref[...], wa_ref[...], preferred_element_type=F32)
    pm = jnp.dot(m_out, wm_ref[...], preferred_element_type=F32)
    merged = _sigmoid(ga_ref[...].astype(F32)) * pa + _sigmoid(gm_ref[...].astype(F32)) * pm
    y = jnp.dot(merged.astype(BF16), wo_ref[...], preferred_element_type=F32)
    z = _layer_norm(alpha * x_ref[...] + y, lng_ref[...], lnb_ref[...])
    xo_ref[...] = z
    xob_ref[...] = z.astype(BF16)


def _merge(a_out, hf, hb, rest, x, wa, wm, wo, lng, lnb, alpha, tm):
    n, d = x.shape
    assert n % tm == 0 and d == 2 * ML_WIDTH and ATT_WIDTH == ML_WIDTH
    const = dict(pipeline_mode=pl.Buffered(1))
    return pl.pallas_call(
        functools.partial(_merge_kernel, alpha=alpha),
        out_shape=(jax.ShapeDtypeStruct((n, d), F32), jax.ShapeDtypeStruct((n, d), BF16)),
        grid=(n // tm,),
        in_specs=[pl.BlockSpec((tm, ATT_WIDTH), lambda i: (i, 0)),
                  pl.BlockSpec((tm, ML_WIDTH), lambda i: (i, 0)),
                  pl.BlockSpec((tm, ML_WIDTH), lambda i: (i, 0)),
                  pl.BlockSpec((tm, ML_WIDTH), lambda i: (i, 3)),
                  pl.BlockSpec((tm, d), lambda i: (i, 2)),
                  pl.BlockSpec((tm, d), lambda i: (i, 3)),
                  pl.BlockSpec((tm, d), lambda i: (i, 0)),
                  pl.BlockSpec((ATT_WIDTH, d), lambda i: (0, 0), **const),
                  pl.BlockSpec((ML_WIDTH, d), lambda i: (0, 0), **const),
                  pl.BlockSpec((d, d), lambda i: (0, 0), **const),
                  pl.BlockSpec((1, d), lambda i: (0, 0)),
                  pl.BlockSpec((1, d), lambda i: (0, 0))],
        out_specs=(pl.BlockSpec((tm, d), lambda i: (i, 0)),
                   pl.BlockSpec((tm, d), lambda i: (i, 0))),
        compiler_params=_cparams(("parallel",), 56),
        name="merge",
    )(a_out, hf, hb, rest, rest, rest, x, wa, wm, wo, lng, lnb)


def _ffn_kernel(xb_ref, w1_ref, w3_ref, w2_ref, x_ref, lng_ref, lnb_ref, xo_ref, xob_ref, acc_ref,
                *, alpha):
    j = pl.program_id(1)

    @pl.when(j == 0)
    def _():
        acc_ref[...] = jnp.zeros(acc_ref.shape, F32)

    xb = xb_ref[...]
    a = jnp.dot(xb, w1_ref[...], preferred_element_type=F32)
    b = jnp.dot(xb, w3_ref[...], preferred_element_type=F32)
    hid = (a * _sigmoid(a) * b).astype(BF16)
    acc_ref[...] += jnp.dot(hid, w2_ref[...], preferred_element_type=F32)

    @pl.when(j == pl.num_programs(1) - 1)
    def _():
        z = _layer_norm(alpha * x_ref[...] + acc_ref[...], lng_ref[...], lnb_ref[...])
        xo_ref[...] = z
        xob_ref[...] = z.astype(BF16)


def _ffn(xb, x, w1, w3, w2, lng, lnb, alpha, tm, tf):
    n, d = x.shape
    f = w1.shape[1]
    assert n % tm == 0 and f % tf == 0
    return pl.pallas_call(
        functools.partial(_ffn_kernel, alpha=alpha),
        out_shape=(jax.ShapeDtypeStruct((n, d), F32), jax.ShapeDtypeStruct((n, d), BF16)),
        grid=(n // tm, f // tf),
        in_specs=[pl.BlockSpec((tm, d), lambda i, j: (i, 0)),
                  pl.BlockSpec((d, tf), lambda i, j: (0, j)),
                  pl.BlockSpec((d, tf), lambda i, j: (0, j)),
                  pl.BlockSpec((tf, d), lambda i, j: (j, 0)),
                  pl.BlockSpec((tm, d), lambda i, j: (i, 0)),
                  pl.BlockSpec((1, d), lambda i, j: (0, 0)),
                  pl.BlockSpec((1, d), lambda i, j: (0, 0))],
        out_specs=(pl.BlockSpec((tm, d), lambda i, j: (i, 0)),
                   pl.BlockSpec((tm, d), lambda i, j: (i, 0))),
        scratch_shapes=[pltpu.VMEM((tm, d), F32)],
        compiler_params=_cparams(("parallel", "arbitrary"), 56),
        name="ffn_dense",
    )(xb, w1, w3, w2, x, lng, lnb)


def _router_kernel(x_ref, wr_ref, tri_ref, ints_ref, flts_ref, cnt_ref, base_ref):
    i = pl.program_id(0)

    @pl.when(i == 0)
    def _():
        base_ref[...] = jnp.zeros(base_ref.shape, F32)

    logits = jnp.dot(x_ref[...], wr_ref[...], preferred_element_type=F32,
                     precision=lax.Precision.HIGHEST)
    lane = lax.broadcasted_iota(I32, logits.shape, 1)
    lanef = lane.astype(F32)
    logits = jnp.where(lane < N_EXPERTS, logits, -jnp.inf)
    v0 = jnp.max(logits, axis=1, keepdims=True)
    i0 = jnp.min(jnp.where(logits == v0, lanef, float(LANES)), axis=1, keepdims=True)
    rest = jnp.where(lanef == i0, -jnp.inf, logits)
    v1 = jnp.max(rest, axis=1, keepdims=True)
    i1 = jnp.min(jnp.where(rest == v1, lanef, float(LANES)), axis=1, keepdims=True)
    e1 = jnp.exp(v1 - v0)
    g0 = 1.0 / (1.0 + e1)
    g1 = e1 / (1.0 + e1)

    sel0 = lanef == i0
    sel1 = lanef == i1
    onehot = jnp.where(sel0 | sel1, 1.0, 0.0)
    before = jnp.dot(tri_ref[...], onehot.astype(BF16), preferred_element_type=F32) + base_ref[...]
    r0 = jnp.sum(jnp.where(sel0, before, 0.0), axis=1, keepdims=True)
    r1 = jnp.sum(jnp.where(sel1, before, 0.0), axis=1, keepdims=True)
    base_ref[...] += jnp.sum(onehot, axis=0, keepdims=True)
    cnt_ref[...] = jnp.broadcast_to(base_ref[...], cnt_ref.shape)

    ints = jnp.where(lane == 0, i0, jnp.where(lane == 1, i1, jnp.where(lane == 2, r0,
                     jnp.where(lane == 3, r1, 0.0))))
    ints_ref[...] = ints.astype(I32)
    flts_ref[...] = jnp.where(lane == 0, g0, jnp.where(lane == 1, g1, 0.0))


def _router(x, wr, tr):
    n, d = x.shape
    assert n % tr == 0
    tri = jnp.asarray(np.tril(np.ones((tr, tr), np.float32), -1), BF16)
    return pl.pallas_call(
        _router_kernel,
        out_shape=(jax.ShapeDtypeStruct((n, LANES), I32),
                   jax.ShapeDtypeStruct((n, LANES), F32),
                   jax.ShapeDtypeStruct((8, LANES), F32)),
        grid=(n // tr,),
        in_specs=[pl.BlockSpec((tr, d), lambda i: (i, 0)),
                  pl.BlockSpec((d, LANES), lambda i: (0, 0)),
                  pl.BlockSpec((tr, tr), lambda i: (0, 0))],
        out_specs=(pl.BlockSpec((tr, LANES), lambda i: (i, 0)),
                   pl.BlockSpec((tr, LANES), lambda i: (i, 0)),
                   pl.BlockSpec((8, LANES), lambda i: (0, 0))),
        scratch_shapes=[pltpu.VMEM((1, LANES), F32)],
        compiler_params=_cparams(("arbitrary",), 32),
        name="moe_router",
    )(x, wr, tri)


def _rows_to_lines(dst_ref, src, n_rows, s):
    for c in range(s):
        dst_ref[pl.ds(c, n_rows, stride=s), :] = src[:, c * LANES:(c + 1) * LANES]


def _line_block(src_ref, c, n_rows, s):
    return src_ref[pl.ds(c, n_rows, stride=s), :]


def _row_dma_loop(make_copy, tt):
    def issue(t, carry):
        for k in range(TOP_K):
            make_copy(t, k).start()
        return carry

    def drain(t, carry):
        for k in range(TOP_K):
            make_copy(t, k).wait()
        return carry

    lax.fori_loop(0, tt, issue, 0)
    lax.fori_loop(0, tt, drain, 0)


def _dispatch_kernel(pos_ref, x_ref, xs_in_ref, xs_ref, st_ref, sem):
    del xs_in_ref
    tt, d = x_ref.shape
    s = d // LANES
    _rows_to_lines(st_ref, x_ref[...], tt, s)

    def copy(t, k):
        src = st_ref.at[pl.ds(pl.multiple_of(t * s, s), s), :]
        dst = xs_ref.at[pl.ds(pl.multiple_of(pos_ref[0, k, t] * s, s), s), :]
        return pltpu.make_async_copy(src, dst, sem)

    _row_dma_loop(copy, tt)


def _dispatch(x, pos, n_rows, tt):
    n, d = x.shape
    s = d // LANES
    xs0 = jnp.zeros((n_rows * s, LANES), x.dtype)
    return pl.pallas_call(
        _dispatch_kernel,
        out_shape=jax.ShapeDtypeStruct(xs0.shape, xs0.dtype),
        grid=(n // tt,),
        in_specs=[pl.BlockSpec((1, TOP_K, tt), lambda i: (i, 0, 0), memory_space=pltpu.SMEM),
                  pl.BlockSpec((tt, d), lambda i: (i, 0)),
                  pl.BlockSpec(memory_space=pl.ANY)],
        out_specs=pl.BlockSpec(memory_space=pl.ANY),
        scratch_shapes=[pltpu.VMEM((tt * s, LANES), x.dtype), pltpu.SemaphoreType.DMA],
        input_output_aliases={2: 0},
        compiler_params=_cparams(("arbitrary",), 32),
        name="moe_dispatch",
    )(pos, x, xs0)


def _moe_ffn_kernel(te_ref, nv_ref, xs_ref, w1_ref, w3_ref, w2_ref, y_ref, xb_ref, acc_ref):
    i = pl.program_id(0)
    j = pl.program_id(1)
    tm, d = xb_ref.shape
    s = d // LANES

    @pl.when(i < nv_ref[0])
    def _():
        @pl.when(j == 0)
        def _():
            for c in range(s):
                xb_ref[:, c * LANES:(c + 1) * LANES] = _line_block(xs_ref, c, tm, s).astype(BF16)
            acc_ref[...] = jnp.zeros(acc_ref.shape, F32)

        xb = xb_ref[...]
        a = jnp.dot(xb, w1_ref[...], preferred_element_type=F32)
        b = jnp.dot(xb, w3_ref[...], preferred_element_type=F32)
        hid = (a * _sigmoid(a) * b).astype(BF16)
        acc_ref[...] += jnp.dot(hid, w2_ref[...], preferred_element_type=F32)

        @pl.when(j == pl.num_programs(1) - 1)
        def _():
            _rows_to_lines(y_ref, acc_ref[...], tm, s)

    @pl.when((i >= nv_ref[0]) & (j == 0))
    def _():
        y_ref[...] = jnp.zeros(y_ref.shape, F32)


def _moe_ffn(xs, d, tile_expert, n_valid, w1, w3, w2, tm, tf):
    s = d // LANES
    p = xs.shape[0] // s
    f = w1.shape[2]
    assert p % tm == 0 and f % tf == 0
    nj = f // tf

    def row_map(i, j, te, nv):
        return (jnp.minimum(i, nv[0] - 1), 0)

    def jj(i, j, nv):
        return jnp.where(i < nv[0], j, nj - 1)

    grid_spec = pltpu.PrefetchScalarGridSpec(
        num_scalar_prefetch=2,
        grid=(p // tm, nj),
        in_specs=[pl.BlockSpec((tm * s, LANES), row_map),
                  pl.BlockSpec((None, d, tf), lambda i, j, te, nv: (te[i], 0, jj(i, j, nv))),
                  pl.BlockSpec((None, d, tf), lambda i, j, te, nv: (te[i], 0, jj(i, j, nv))),
                  pl.BlockSpec((None, tf, d), lambda i, j, te, nv: (te[i], jj(i, j, nv), 0))],
        out_specs=pl.BlockSpec((tm * s, LANES), lambda i, j, te, nv: (i, 0)),
        scratch_shapes=[pltpu.VMEM((tm, d), BF16), pltpu.VMEM((tm, d), F32)],
    )
    return pl.pallas_call(
        _moe_ffn_kernel,
        out_shape=jax.ShapeDtypeStruct((p * s, LANES), F32),
        grid_spec=grid_spec,
        compiler_params=_cparams(("arbitrary", "arbitrary"), 56),
        name="moe_ffn",
    )(tile_expert, n_valid, xs, w1, w3, w2)


def _combine_kernel(pos_ref, y_ref, g_ref, x_ref, lng_ref, lnb_ref, xo_ref,
                    buf0_ref, buf1_ref, z_ref, sem, *, alpha):
    tt, d = x_ref.shape
    s = d // LANES

    def copy(t, k):
        buf = buf0_ref if k == 0 else buf1_ref
        src = y_ref.at[pl.ds(pl.multiple_of(pos_ref[0, k, t] * s, s), s), :]
        return pltpu.make_async_copy(src, buf.at[pl.ds(pl.multiple_of(t * s, s), s), :], sem)

    _row_dma_loop(copy, tt)
    g0 = g_ref[:, 0:1]
    g1 = g_ref[:, 1:2]
    for c in range(s):
        cols = slice(c * LANES, (c + 1) * LANES)
        f = g0 * _line_block(buf0_ref, c, tt, s) + g1 * _line_block(buf1_ref, c, tt, s)
        z_ref[:, cols] = alpha * x_ref[:, cols] + f
    xo_ref[...] = _layer_norm(z_ref[...], lng_ref[...], lnb_ref[...])


def _combine(y, pos, gates, x, lng, lnb, alpha, tt):
    n, d = x.shape
    s = d // LANES
    return pl.pallas_call(
        functools.partial(_combine_kernel, alpha=alpha),
        out_shape=jax.ShapeDtypeStruct((n, d), F32),
        grid=(n // tt,),
        in_specs=[pl.BlockSpec((1, TOP_K, tt), lambda i: (i, 0, 0), memory_space=pltpu.SMEM),
                  pl.BlockSpec(memory_space=pl.ANY),
                  pl.BlockSpec((tt, LANES), lambda i: (i, 0)),
                  pl.BlockSpec((tt, d), lambda i: (i, 0)),
                  pl.BlockSpec((1, d), lambda i: (0, 0)),
                  pl.BlockSpec((1, d), lambda i: (0, 0))],
        out_specs=pl.BlockSpec((tt, d), lambda i: (i, 0)),
        scratch_shapes=[pltpu.VMEM((tt * s, LANES), F32),
                        pltpu.VMEM((tt * s, LANES), F32),
                        pltpu.VMEM((tt, d), F32),
                        pltpu.SemaphoreType.DMA],
        compiler_params=_cparams(("arbitrary",), 32),
        name="moe_combine",
    )(pos, y, gates, x, lng, lnb)


def _moe(x, wr, w1, w3, w2, lng, lnb, alpha, tm, tf, tr, tt):
    n, d = x.shape
    ints, flts, cnt = _router(x, wr, tr)
    e0, e1, r0, r1 = ints[:, 0], ints[:, 1], ints[:, 2], ints[:, 3]
    counts = cnt[0, :N_EXPERTS].astype(I32)

    tiles = (counts + tm - 1) // tm
    tile_end = jnp.cumsum(tiles)
    offs = (tile_end - tiles) * tm
    n_tiles = (TOP_K * n) // tm + N_EXPERTS
    n_valid = tile_end[-1:]
    tile_expert = jnp.sum(jnp.arange(n_tiles, dtype=I32)[:, None] >= tile_end[None, :], axis=1)
    tile_expert = jnp.minimum(tile_expert, N_EXPERTS - 1).astype(I32)
    tile_expert = jnp.where(jnp.arange(n_tiles) < n_valid[0], tile_expert,
                            tile_expert[jnp.maximum(n_valid[0] - 1, 0)])
    pos = jnp.stack([offs[e0] + r0, offs[e1] + r1], axis=0)
    pos = pos.reshape(TOP_K, n // tt, tt).transpose(1, 0, 2)

    xs = _dispatch(x, pos, n_tiles * tm, tt)
    y = _moe_ffn(xs, d, tile_expert, n_valid.astype(I32), w1, w3, w2, tm, tf)
    return _combine(y, pos, flts, x, lng[None, :], lnb[None, :], alpha, tt)


def _rope_tables(t_max):
    half = ATT_HEAD_DIM // 2
    pos = np.arange(t_max)
    pos_r = (pos // GRID_W).astype(np.float32)
    pos_c = (pos % GRID_W).astype(np.float32)
    inv = jnp.asarray(ROPE_THETA, F32) ** (-jnp.arange(0, half, 2, dtype=F32) / half)
    ang_r = jnp.asarray(pos_r)[:, None] * inv
    ang_c = jnp.asarray(pos_c)[:, None] * inv
    cr, sr, cc, sc = jnp.cos(ang_r), jnp.sin(ang_r), jnp.cos(ang_c), jnp.sin(ang_c)
    zero = jnp.zeros_like(sr)
    cos = jnp.concatenate([cr, cr, cc, cc], axis=1)
    sin_up = jnp.concatenate([-sr, zero, -sc, zero], axis=1)
    sin_dn = jnp.concatenate([zero, sr, zero, sc], axis=1)
    return cos, sin_up, sin_dn


def _split_w_in(w_in_l):
    sizes = (ATT_WIDTH, KV_WIDTH, KV_WIDTH, ML_WIDTH, ML_WIDTH, ML_WIDTH, ML_WIDTH, N_ML_GATES)
    o = np.cumsum((0,) + sizes)
    d = w_in_l.shape[0]
    w_gate = jnp.concatenate([w_in_l[:, o[7]:o[8]], jnp.zeros((d, LANES - N_ML_GATES), w_in_l.dtype)], axis=1)
    w_att = jnp.concatenate([w_in_l[:, o[0]:o[3]], w_gate], axis=1).astype(BF16)
    w_rest = jnp.concatenate([w_in_l[:, o[3]:o[7]], w_in_l[:, o[8]:]], axis=1).astype(BF16)
    return w_att, w_rest


def kernel(x_prompt, x_sample, w_in, b_gates, q_gain, k_gain, w_att_br, w_ml_br, w_o, ln1_g, ln1_b,
           w1_d, w3_d, w2_d, router, e_w1, e_w3, e_w2, ln2_g, ln2_b):
    depth = w_in.shape[0]
    d = x_prompt.shape[-1]
    seqs = ((x_prompt.shape[0], x_prompt.shape[1]), (x_sample.shape[0], x_sample.shape[1]))
    n0 = seqs[0][0] * seqs[0][1]
    alpha = float((2 * depth) ** 0.25)

    x = jnp.concatenate([x_prompt.reshape(-1, d), x_sample.reshape(-1, d)], axis=0)
    xb = x.astype(BF16)
    cos, sin_up, sin_dn = _rope_tables(max(seqs[0][1], seqs[1][1]))

    for l in range(depth):
        w_att, w_rest = _split_w_in(w_in[l])
        bias = jnp.concatenate([b_gates[l], jnp.zeros((LANES - N_ML_GATES,), F32)])[None, :]
        gain = jnp.concatenate([jnp.broadcast_to(q_gain[l], (ATT_HEADS, ATT_HEAD_DIM)),
                                jnp.broadcast_to(k_gain[l], (ATT_KV_HEADS, ATT_HEAD_DIM)),
                                jnp.zeros((16 - ATT_HEADS - ATT_KV_HEADS, ATT_HEAD_DIM), F32)], axis=0)

        T = TILES
        qkv, gates = _att_proj(xb, w_att, bias, gain, cos, sin_up, sin_dn, seqs, tm=T["att_proj_tm"])
        rest = _matmul(xb, w_rest, BF16, tm=T["proj_tm"], tn=T["proj_tn"])

        a_out = jnp.concatenate(
            [_attention(qkv, 0, seqs[0][0], seqs[0][1], bq=min(T["attn_bq"], seqs[0][1]), tk=T["attn_tk"]),
             _attention(qkv, n0, seqs[1][0], seqs[1][1], bq=min(T["attn_bq"], seqs[1][1]), tk=T["attn_tk"])],
            axis=0)
        hf, hb = _mlstm(rest, rest[:, ML_WIDTH:2 * ML_WIDTH].T, gates[:, :N_ML_GATES].T, seqs)

        x, xb = _merge(a_out, hf, hb, rest, x, w_att_br[l].astype(BF16), w_ml_br[l].astype(BF16),
                       w_o[l].astype(BF16), ln1_g[l][None, :], ln1_b[l][None, :], alpha, tm=T["merge_tm"])

        j = l // 2
        if l % 2 == 0:
            x, xb = _ffn(xb, x, w1_d[j].astype(BF16), w3_d[j].astype(BF16), w2_d[j].astype(BF16),
                         ln2_g[l][None, :], ln2_b[l][None, :], alpha, tm=T["ffn_tm"], tf=T["ffn_tf"])
        else:
            wr = jnp.concatenate([router[j], jnp.zeros((d, LANES - N_EXPERTS), F32)], axis=1)
            x = _moe(x, wr, e_w1[j].astype(BF16), e_w3[j].astype(BF16), e_w2[j].astype(BF16),
                     ln2_g[l], ln2_b[l], alpha, tm=T["moe_tm"], tf=T["moe_tf"], tr=T["router_tr"],
                     tt=T["moe_tt"])
            xb = x.astype(BF16)

    return (x[:n0].reshape(x_prompt.shape), x[n0:].reshape(x_sample.shape))
```

```python
import functools
import math

import numpy as np
import jax
import jax.numpy as jnp
from jax import lax
from jax.experimental import pallas as pl
from jax.experimental.pallas import tpu as pltpu

F32 = jnp.float32
BF16 = jnp.bfloat16
I32 = jnp.int32

GRID_W = 64
ATT_HEADS = 8
ATT_KV_HEADS = 2
ATT_HEAD_DIM = 128
ATT_GROUP = ATT_HEADS // ATT_KV_HEADS
ROPE_THETA = 10000.0
QK_EPS = 1e-6
ML_HEADS = 4
ML_HEAD_DIM = 256
ML_CHUNK = 128
ATT_WIDTH = ATT_HEADS * ATT_HEAD_DIM
KV_WIDTH = ATT_KV_HEADS * ATT_HEAD_DIM
ML_WIDTH = ML_HEADS * ML_HEAD_DIM
N_ML_GATES = 4 * ML_HEADS
N_EXPERTS = 8
TOP_K = 2
LN_EPS = 1e-5

LANES = 128
V7X_VMEM_BYTES = 64 * 2 ** 20
MIB = 2 ** 20

TILES = dict(att_proj_tm=512, proj_tm=1024, proj_tn=1024, attn_bq=1024, attn_tk=512, merge_tm=256,
             ffn_tm=512, ffn_tf=512, moe_tm=512, moe_tf=512, router_tr=512, moe_tt=256)


def _cparams(semantics, vmem_mib):
    assert vmem_mib * MIB < V7X_VMEM_BYTES
    return pltpu.CompilerParams(dimension_semantics=semantics, vmem_limit_bytes=vmem_mib * MIB)


def _sigmoid(x):
    return 1.0 / (1.0 + jnp.exp(-x))


def _log_sigmoid(x):
    return jnp.minimum(x, 0.0) - jnp.log1p(jnp.exp(-jnp.abs(x)))


def _mean_last(z, n_axes):
    count = 1
    for ax in range(z.ndim - 1, z.ndim - 1 - n_axes, -1):
        count *= z.shape[ax]
        z = jnp.sum(z, axis=ax, keepdims=True)
    return z * (1.0 / count)


def _layer_norm(z, g, b, n_axes=1):
    mu = _mean_last(z, n_axes)
    zc = z - mu
    var = _mean_last(zc * zc, n_axes)
    return zc * lax.rsqrt(var + LN_EPS) * g + b


def _att_proj_kernel(x_ref, w_ref, bias_ref, gain_ref, cos_ref, sa_ref, sb_ref,
                     qkv_ref, gates_ref, *, n_heads, n_norm_heads):
    x = x_ref[...]
    cos = cos_ref[...]
    sa = sa_ref[...]
    sb = sb_ref[...]
    ones = jnp.ones((x.shape[0], LANES), qkv_ref.dtype)
    for h in range(n_heads):
        if h % 2 == 0:
            pair = jnp.dot(x, w_ref[:, h * LANES:(h + 2) * LANES], preferred_element_type=F32)
        a = pair[:, (h % 2) * LANES:(h % 2 + 1) * LANES]
        if h < n_norm_heads:
            a = a * lax.rsqrt(jnp.mean(a * a, axis=-1, keepdims=True) + QK_EPS) * gain_ref[h:h + 1, :]
            a = a * cos + pltpu.roll(a, 96, 1) * sa + pltpu.roll(a, 32, 1) * sb
            qkv_ref[:, h * LANES:(h + 1) * LANES] = a.astype(qkv_ref.dtype)
        else:
            c = n_norm_heads + 2 * (h - n_norm_heads)
            qkv_ref[:, c * LANES:(c + 1) * LANES] = a.astype(qkv_ref.dtype)
            qkv_ref[:, (c + 1) * LANES:(c + 2) * LANES] = ones
    gates_ref[...] = (jnp.dot(x, w_ref[:, n_heads * LANES:], preferred_element_type=F32)
                      + bias_ref[...])


def _att_proj(xb, w, bias, gain, cos, sa, sb, seqs, tm):
    n, d = xb.shape
    n_heads = (ATT_WIDTH + 2 * KV_WIDTH) // LANES
    ncol = n_heads * LANES + LANES
    assert w.shape == (d, ncol)
    n_out = (n_heads + ATT_KV_HEADS) * LANES

    (b0, t0), (b1, t1) = seqs
    assert t0 % tm == 0 and t1 % tm == 0
    nb0 = b0 * t0 // tm

    def pos_map(i):
        return (jnp.where(i < nb0, i % (t0 // tm), (i - nb0) % (t1 // tm)), 0)

    return pl.pallas_call(
        functools.partial(_att_proj_kernel, n_heads=n_heads, n_norm_heads=ATT_HEADS + ATT_KV_HEADS),
        out_shape=(jax.ShapeDtypeStruct((n, n_out), BF16),
                   jax.ShapeDtypeStruct((n, LANES), F32)),
        grid=(n // tm,),
        in_specs=[pl.BlockSpec((tm, d), lambda i: (i, 0)),
                  pl.BlockSpec((d, ncol), lambda i: (0, 0)),
                  pl.BlockSpec((1, LANES), lambda i: (0, 0)),
                  pl.BlockSpec((16, LANES), lambda i: (0, 0)),
                  pl.BlockSpec((tm, LANES), pos_map),
                  pl.BlockSpec((tm, LANES), pos_map),
                  pl.BlockSpec((tm, LANES), pos_map)],
        out_specs=(pl.BlockSpec((tm, n_out), lambda i: (i, 0)),
                   pl.BlockSpec((tm, LANES), lambda i: (i, 0))),
        compiler_params=_cparams(("parallel",), 48),
        name="att_proj",
    )(xb, w, bias, gain, cos, sa, sb)


def _matmul_kernel(x_ref, w_ref, o_ref):
    o_ref[...] = jnp.dot(x_ref[...], w_ref[...], preferred_element_type=F32).astype(o_ref.dtype)


def _matmul(xb, w, out_dtype, tm, tn):
    n, d = xb.shape
    ncol = w.shape[1]
    assert n % tm == 0 and ncol % tn == 0
    return pl.pallas_call(
        _matmul_kernel,
        out_shape=jax.ShapeDtypeStruct((n, ncol), out_dtype),
        grid=(ncol // tn, n // tm),
        in_specs=[pl.BlockSpec((tm, d), lambda j, i: (i, 0)),
                  pl.BlockSpec((d, tn), lambda j, i: (0, j))],
        out_specs=pl.BlockSpec((tm, tn), lambda j, i: (i, j)),
        compiler_params=_cparams(("parallel", "parallel"), 48),
        name="matmul",
    )(xb, w)


def _attn_kernel(q_ref, k_ref, v_ref, o_ref, m_ref, acc_ref, *, tk, scale):
    bq = q_ref.shape[0]
    t = k_ref.shape[0]
    c = scale * math.log2(math.e)
    m_ref[...] = jnp.full(m_ref.shape, -jnp.inf, F32)
    acc_ref[...] = jnp.zeros(acc_ref.shape, F32)

    def body(ci, carry):
        start = pl.multiple_of(ci * tk, tk)
        k = k_ref[pl.ds(start, tk), :]
        v = v_ref[pl.ds(start, tk), :]
        for h in range(ATT_GROUP):
            r = slice(h * bq, (h + 1) * bq)
            s = lax.dot_general(q_ref[:, h * LANES:(h + 1) * LANES], k, (((1,), (1,)), ((), ())),
                                preferred_element_type=F32)
            m_prev = m_ref[r, :]
            m_new = jnp.maximum(m_prev, jnp.max(s, axis=1, keepdims=True))
            p = jnp.exp2((s - jnp.tile(m_new, (1, tk // LANES))) * c)
            alpha = jnp.exp2((m_prev - m_new) * c)
            pv = jnp.dot(p.astype(BF16), v, preferred_element_type=F32)
            acc_ref[r, :] = jnp.tile(alpha, (1, 2)) * acc_ref[r, :] + pv
            m_ref[r, :] = m_new
        return carry

    lax.fori_loop(0, t // tk, body, 0, unroll=2)
    for h in range(ATT_GROUP):
        r = slice(h * bq, (h + 1) * bq)
        o_ref[:, h * LANES:(h + 1) * LANES] = (acc_ref[r, :LANES] / acc_ref[r, LANES:]).astype(o_ref.dtype)


def _attention(qkv, tok0, nseq, t, bq, tk):
    assert t % bq == 0 and t % tk == 0 and tok0 % t == 0
    qb0 = tok0 // bq
    sb0 = tok0 // t
    nq = t // bq
    gw = ATT_GROUP * LANES
    v_blk0 = (ATT_HEADS + ATT_KV_HEADS) // 2
    return pl.pallas_call(
        functools.partial(_attn_kernel, tk=tk, scale=ATT_HEAD_DIM ** -0.5),
        out_shape=jax.ShapeDtypeStruct((nseq * t, ATT_WIDTH), BF16),
        grid=(nseq, ATT_KV_HEADS, nq),
        in_specs=[pl.BlockSpec((bq, gw), lambda b, g, i: (qb0 + b * nq + i, g)),
                  pl.BlockSpec((t, LANES), lambda b, g, i: (sb0 + b, ATT_HEADS + g)),
                  pl.BlockSpec((t, 2 * LANES), lambda b, g, i: (sb0 + b, v_blk0 + g))],
        out_specs=pl.BlockSpec((bq, gw), lambda b, g, i: (b * nq + i, g)),
        scratch_shapes=[pltpu.VMEM((ATT_GROUP * bq, LANES), F32),
                        pltpu.VMEM((ATT_GROUP * bq, 2 * LANES), F32)],
        compiler_params=_cparams(("parallel", "parallel", "parallel"), 40),
        name="attention",
    )(qkv, qkv, qkv)


def _mlstm_chain(d, hd, q, k, kt, v, gt, h_ref, c_ref, n_ref, m_ref):
    L, dk = q.shape
    ci = d * ML_HEADS + hd
    cf = 2 * ML_HEADS + ci
    ig_row = gt[ci:ci + 1, :]
    logf_row = _log_sigmoid(gt[cf:cf + 1, :])

    row = lax.broadcasted_iota(I32, (L, L), 0)
    col = lax.broadcasted_iota(I32, (L, L), 1)
    mask = (col <= row) if d == 0 else (col >= row)
    b_col = jnp.sum(jnp.where(mask, logf_row, 0.0), axis=1, keepdims=True)
    b_row = jnp.sum(jnp.where(row == col, b_col, 0.0), axis=0, keepdims=True)
    gtot = jnp.sum(logf_row, axis=1, keepdims=True)

    m_prev = m_ref[...]
    scale = dk ** -0.5
    assert math.log2(dk) % 2 == 0

    a_row = gtot - b_row + ig_row
    m_loc = jnp.max(a_row, axis=1, keepdims=True)
    w_row = jnp.exp(a_row - m_loc) * scale
    c_loc = jnp.dot((kt.astype(F32) * w_row).astype(BF16), v, preferred_element_type=F32)
    n_loc = jnp.dot(jnp.broadcast_to(w_row, (8, L)).astype(BF16), k, preferred_element_type=F32)[0:1, :]

    dmat = jnp.where(mask, b_col - b_row + ig_row, -jnp.inf)
    inter = b_col + m_prev
    m_j = jnp.maximum(jnp.max(dmat, axis=1, keepdims=True), inter)
    qk = jnp.dot(q, kt, preferred_element_type=F32)
    s = qk * (jnp.exp(dmat - m_j) * scale)
    s_int = jnp.exp(inter - m_j)
    num = (jnp.dot(s.astype(BF16), v, preferred_element_type=F32)
           + s_int * jnp.dot(q, c_ref[...].astype(BF16), preferred_element_type=F32))
    den = (jnp.sum(s, axis=1, keepdims=True)
           + s_int * jnp.sum(q.astype(F32) * n_ref[...], axis=1, keepdims=True))
    h_ref[...] = num / jnp.maximum(jnp.abs(den), jnp.exp(-m_j))

    m_new = jnp.maximum(gtot + m_prev, m_loc)
    s_old = jnp.exp(gtot + m_prev - m_new)
    s_new = jnp.exp(m_loc - m_new)
    c_ref[...] = s_old * c_ref[...] + s_new * c_loc
    n_ref[...] = s_old * n_ref[...] + s_new * n_loc
    m_ref[...] = m_new


def _mlstm_kernel(first_ref, qf_ref, kf_ref, vf_ref, ktf_ref, gtf_ref, qb_ref, kb_ref, vb_ref, ktb_ref,
                  gtb_ref, hf_ref, hb_ref, c_ref, n_ref, m_ref):
    i = pl.program_id(0)
    dirs = ((qf_ref, kf_ref, vf_ref, ktf_ref, gtf_ref, hf_ref),
            (qb_ref, kb_ref, vb_ref, ktb_ref, gtb_ref, hb_ref))
    for d in range(2):
        @pl.when(first_ref[d, i] == 1)
        def _():
            for hd in range(ML_HEADS):
                s = d * ML_HEADS + hd
                c_ref[s] = jnp.zeros(c_ref.shape[1:], F32)
                n_ref[s] = jnp.zeros(n_ref.shape[1:], F32)
                m_ref[s] = jnp.zeros(m_ref.shape[1:], F32)

    for d, (q_ref, k_ref, v_ref, kt_ref, gt_ref, h_ref) in enumerate(dirs):
        gt = gt_ref[...]
        for hd in range(ML_HEADS):
            s = d * ML_HEADS + hd
            cols = slice(hd * ML_HEAD_DIM, (hd + 1) * ML_HEAD_DIM)
            _mlstm_chain(d, hd, q_ref[:, cols], k_ref[:, cols], kt_ref[cols, :], v_ref[:, cols], gt,
                         h_ref.at[:, cols], c_ref.at[s], n_ref.at[s], m_ref.at[s])


def _mlstm(rest, k_t, gates_t, seqs):
    n = rest.shape[0]
    L = ML_CHUNK
    nc = n // L
    first_f = np.zeros((nc,), np.int32)
    last_f = np.zeros((nc,), np.int32)
    tok = 0
    for (b, t) in seqs:
        for _ in range(b):
            first_f[tok // L] = 1
            last_f[(tok + t) // L - 1] = 1
            tok += t
    first = np.stack([first_f, last_f[::-1]]).astype(np.int32)
    W = ML_WIDTH
    nstate = 2 * ML_HEADS

    def fwd(c):
        return lambda i, fst: (i, c)

    def bwd(c):
        return lambda i, fst: (nc - 1 - i, c)

    def specs(m, mt):
        return [pl.BlockSpec((L, W), m(0)), pl.BlockSpec((L, W), m(1)), pl.BlockSpec((L, W), m(2)),
                pl.BlockSpec((W, L), mt), pl.BlockSpec((N_ML_GATES, L), mt)]

    grid_spec = pltpu.PrefetchScalarGridSpec(
        num_scalar_prefetch=1,
        grid=(nc,),
        in_specs=(specs(fwd, lambda i, fst: (0, i)) + specs(bwd, lambda i, fst: (0, nc - 1 - i))),
        out_specs=(pl.BlockSpec((L, W), fwd(0)), pl.BlockSpec((L, W), bwd(0))),
        scratch_shapes=[pltpu.VMEM((nstate, ML_HEAD_DIM, ML_HEAD_DIM), F32),
                        pltpu.VMEM((nstate, 1, ML_HEAD_DIM), F32),
                        pltpu.VMEM((nstate, 1, 1), F32)],
    )
    return pl.pallas_call(
        _mlstm_kernel,
        out_shape=(jax.ShapeDtypeStruct((n, W), F32), jax.ShapeDtypeStruct((n, W), F32)),
        grid_spec=grid_spec,
        compiler_params=_cparams(("arbitrary",), 32),
        name="mlstm",
    )(jnp.asarray(first), rest, rest, rest, k_t, gates_t, rest, rest, rest, k_t, gates_t)


def _merge_kernel(a_ref, hf_ref, hb_ref, mo_ref, ga_ref, gm_ref, x_ref, wa_ref, wm_ref, wo_ref,
                  lng_ref, lnb_ref, xo_ref, xob_ref, *, alpha):
    m_out = (_sigmoid(mo_ref[...].astype(F32)) * (hf_ref[...] + hb_ref[...])).astype(BF16)
    pa = jnp.dot(a_ref[...], wa_ref[...], preferred_element_type=F32)
    pm = jnp.dot(m_out, wm_ref[...], preferred_element_type=F32)
    merged = _sigmoid(ga_ref[...].astype(F32)) * pa + _sigmoid(gm_ref[...].astype(F32)) * pm
    y = jnp.dot(merged.astype(BF16), wo_ref[...], preferred_element_type=F32)
    z = _layer_norm(alpha * x_ref[...] + y, lng_ref[...], lnb_ref[...])
    xo_ref[...] = z
    xob_ref[...] = z.astype(BF16)


def _merge(a_out, hf, hb, rest, x, wa, wm, wo, lng, lnb, alpha, tm):
    n, d = x.shape
    assert n % tm == 0 and d == 2 * ML_WIDTH and ATT_WIDTH == ML_WIDTH
    const = dict(pipeline_mode=pl.Buffered(1))
    return pl.pallas_call(
        functools.partial(_merge_kernel, alpha=alpha),
        out_shape=(jax.ShapeDtypeStruct((n, d), F32), jax.ShapeDtypeStruct((n, d), BF16)),
        grid=(n // tm,),
        in_specs=[pl.BlockSpec((tm, ATT_WIDTH), lambda i: (i, 0)),
                  pl.BlockSpec((tm, ML_WIDTH), lambda i: (i, 0)),
                  pl.BlockSpec((tm, ML_WIDTH), lambda i: (i, 0)),
                  pl.BlockSpec((tm, ML_WIDTH), lambda i: (i, 3)),
                  pl.BlockSpec((tm, d), lambda i: (i, 2)),
                  pl.BlockSpec((tm, d), lambda i: (i, 3)),
                  pl.BlockSpec((tm, d), lambda i: (i, 0)),
                  pl.BlockSpec((ATT_WIDTH, d), lambda i: (0, 0), **const),
                  pl.BlockSpec((ML_WIDTH, d), lambda i: (0, 0), **const),
                  pl.BlockSpec((d, d), lambda i: (0, 0), **const),
                  pl.BlockSpec((1, d), lambda i: (0, 0)),
                  pl.BlockSpec((1, d), lambda i: (0, 0))],
        out_specs=(pl.BlockSpec((tm, d), lambda i: (i, 0)),
                   pl.BlockSpec((tm, d), lambda i: (i, 0))),
        compiler_params=_cparams(("parallel",), 56),
        name="merge",
    )(a_out, hf, hb, rest, rest, rest, x, wa, wm, wo, lng, lnb)


def _ffn_kernel(xb_ref, w1_ref, w3_ref, w2_ref, x_ref, lng_ref, lnb_ref, *refs, alpha, n_cast):
    cast_in = refs[:n_cast]
    xo_ref, xob_ref = refs[n_cast:n_cast + 2]
    cast_out = refs[n_cast + 2:2 * n_cast + 2]
    acc_ref = refs[-1]
    j = pl.program_id(1)

    for src, dst in zip(cast_in, cast_out):
        dst[...] = src[...].astype(BF16)

    @pl.when(j == 0)
    def _():
        acc_ref[...] = jnp.zeros(acc_ref.shape, F32)

    xb = xb_ref[...]
    a = jnp.dot(xb, w1_ref[...], preferred_element_type=F32)
    b = jnp.dot(xb, w3_ref[...], preferred_element_type=F32)
    hid = (a * _sigmoid(a) * b).astype(BF16)
    acc_ref[...] += jnp.dot(hid, w2_ref[...], preferred_element_type=F32)

    @pl.when(j == pl.num_programs(1) - 1)
    def _():
        z = _layer_norm(alpha * x_ref[...] + acc_ref[...], lng_ref[...], lnb_ref[...])
        xo_ref[...] = z
        xob_ref[...] = z.astype(BF16)


def _cast_block_rows(rows, steps):
    rb = 16 * pl.cdiv(pl.cdiv(rows, steps), 16)
    while rows % rb:
        rb += 16
    return rb


def _ffn(xb, x, w1, w3, w2, lng, lnb, alpha, tm, tf, to_cast=()):
    n, d = x.shape
    f = w1.shape[1]
    assert n % tm == 0 and f % tf == 0
    nj = f // tf
    steps = (n // tm) * nj
    cast_specs = []
    for a in to_cast:
        rb = _cast_block_rows(a.shape[0], steps)
        last = a.shape[0] // rb - 1
        cast_specs.append(pl.BlockSpec((rb, a.shape[1]),
                                       lambda i, j, last=last: (jnp.minimum(i * nj + j, last), 0)))
    outs = pl.pallas_call(
        functools.partial(_ffn_kernel, alpha=alpha, n_cast=len(to_cast)),
        out_shape=(jax.ShapeDtypeStruct((n, d), F32), jax.ShapeDtypeStruct((n, d), BF16),
                   *[jax.ShapeDtypeStruct(a.shape, BF16) for a in to_cast]),
        grid=(n // tm, nj),
        in_specs=[pl.BlockSpec((tm, d), lambda i, j: (i, 0)),
                  pl.BlockSpec((d, tf), lambda i, j: (0, j)),
                  pl.BlockSpec((d, tf), lambda i, j: (0, j)),
                  pl.BlockSpec((tf, d), lambda i, j: (j, 0)),
                  pl.BlockSpec((tm, d), lambda i, j: (i, 0)),
                  pl.BlockSpec((1, d), lambda i, j: (0, 0)),
                  pl.BlockSpec((1, d), lambda i, j: (0, 0)),
                  *cast_specs],
        out_specs=(pl.BlockSpec((tm, d), lambda i, j: (i, 0)),
                   pl.BlockSpec((tm, d), lambda i, j: (i, 0)),
                   *cast_specs),
        scratch_shapes=[pltpu.VMEM((tm, d), F32)],
        compiler_params=_cparams(("arbitrary", "arbitrary"), 56),
        name="ffn_dense",
    )(xb, w1, w3, w2, x, lng, lnb, *to_cast)
    return outs[0], outs[1], outs[2:]


def _router_kernel(x_ref, wr_ref, tri_ref, ints_ref, flts_ref, cnt_ref, base_ref):
    i = pl.program_id(0)

    @pl.when(i == 0)
    def _():
        base_ref[...] = jnp.zeros(base_ref.shape, F32)

    logits = jnp.dot(x_ref[...], wr_ref[...], preferred_element_type=F32,
                     precision=lax.Precision.HIGHEST)
    lane = lax.broadcasted_iota(I32, logits.shape, 1)
    lanef = lane.astype(F32)
    logits = jnp.where(lane < N_EXPERTS, logits, -jnp.inf)
    v0 = jnp.max(logits, axis=1, keepdims=True)
    i0 = jnp.min(jnp.where(logits == v0, lanef, float(LANES)), axis=1, keepdims=True)
    rest = jnp.where(lanef == i0, -jnp.inf, logits)
    v1 = jnp.max(rest, axis=1, keepdims=True)
    i1 = jnp.min(jnp.where(rest == v1, lanef, float(LANES)), axis=1, keepdims=True)
    e1 = jnp.exp(v1 - v0)
    g0 = 1.0 / (1.0 + e1)
    g1 = e1 / (1.0 + e1)

    sel0 = lanef == i0
    sel1 = lanef == i1
    onehot = jnp.where(sel0 | sel1, 1.0, 0.0)
    before = jnp.dot(tri_ref[...], onehot.astype(BF16), preferred_element_type=F32) + base_ref[...]
    r0 = jnp.sum(jnp.where(sel0, before, 0.0), axis=1, keepdims=True)
    r1 = jnp.sum(jnp.where(sel1, before, 0.0), axis=1, keepdims=True)
    base_ref[...] += jnp.sum(onehot, axis=0, keepdims=True)
    cnt_ref[...] = jnp.broadcast_to(base_ref[...], cnt_ref.shape)

    ints = jnp.where(lane == 0, i0, jnp.where(lane == 1, i1, jnp.where(lane == 2, r0,
                     jnp.where(lane == 3, r1, 0.0))))
    ints_ref[...] = ints.astype(I32)
    flts_ref[...] = jnp.where(lane == 0, g0, jnp.where(lane == 1, g1, 0.0))


def _router(x, wr, tr):
    n, d = x.shape
    assert n % tr == 0
    tri = jnp.asarray(np.tril(np.ones((tr, tr), np.float32), -1), BF16)
    return pl.pallas_call(
        _router_kernel,
        out_shape=(jax.ShapeDtypeStruct((n, LANES), I32),
                   jax.ShapeDtypeStruct((n, LANES), F32),
                   jax.ShapeDtypeStruct((8, LANES), F32)),
        grid=(n // tr,),
        in_specs=[pl.BlockSpec((tr, d), lambda i: (i, 0)),
                  pl.BlockSpec((d, LANES), lambda i: (0, 0)),
                  pl.BlockSpec((tr, tr), lambda i: (0, 0))],
        out_specs=(pl.BlockSpec((tr, LANES), lambda i: (i, 0)),
                   pl.BlockSpec((tr, LANES), lambda i: (i, 0)),
                   pl.BlockSpec((8, LANES), lambda i: (0, 0))),
        scratch_shapes=[pltpu.VMEM((1, LANES), F32)],
        compiler_params=_cparams(("arbitrary",), 32),
        name="moe_router",
    )(x, wr, tri)


def _rows_to_lines(dst_ref, src, n_rows, s):
    for c in range(s):
        dst_ref[pl.ds(c, n_rows, stride=s), :] = src[:, c * LANES:(c + 1) * LANES]


def _line_block(src_ref, c, n_rows, s):
    return src_ref[pl.ds(c, n_rows, stride=s), :]


def _row_dma_loop(make_copy, tt):
    def issue(t, carry):
        for k in range(TOP_K):
            make_copy(t, k).start()
        return carry

    def drain(t, carry):
        for k in range(TOP_K):
            make_copy(t, k).wait()
        return carry

    lax.fori_loop(0, tt, issue, 0)
    lax.fori_loop(0, tt, drain, 0)


def _dispatch_kernel(pos_ref, x_ref, xs_in_ref, xs_ref, st_ref, sem):
    del xs_in_ref
    tt, d = x_ref.shape
    s = d // LANES
    _rows_to_lines(st_ref, x_ref[...], tt, s)

    def copy(t, k):
        src = st_ref.at[pl.ds(pl.multiple_of(t * s, s), s), :]
        dst = xs_ref.at[pl.ds(pl.multiple_of(pos_ref[0, k, t] * s, s), s), :]
        return pltpu.make_async_copy(src, dst, sem)

    _row_dma_loop(copy, tt)


def _dispatch(x, pos, n_rows, tt):
    n, d = x.shape
    s = d // LANES
    xs0 = jnp.zeros((n_rows * s, LANES), x.dtype)
    return pl.pallas_call(
        _dispatch_kernel,
        out_shape=jax.ShapeDtypeStruct(xs0.shape, xs0.dtype),
        grid=(n // tt,),
        in_specs=[pl.BlockSpec((1, TOP_K, tt), lambda i: (i, 0, 0), memory_space=pltpu.SMEM),
                  pl.BlockSpec((tt, d), lambda i: (i, 0)),
                  pl.BlockSpec(memory_space=pl.ANY)],
        out_specs=pl.BlockSpec(memory_space=pl.ANY),
        scratch_shapes=[pltpu.VMEM((tt * s, LANES), x.dtype), pltpu.SemaphoreType.DMA],
        input_output_aliases={2: 0},
        compiler_params=_cparams(("arbitrary",), 32),
        name="moe_dispatch",
    )(pos, x, xs0)


def _moe_ffn_kernel(te_ref, nv_ref, xs_ref, w1_ref, w3_ref, w2_ref, y_ref, xb_ref, acc_ref):
    i = pl.program_id(0)
    j = pl.program_id(1)
    tm, d = xb_ref.shape
    s = d // LANES

    @pl.when(i < nv_ref[0])
    def _():
        @pl.when(j == 0)
        def _():
            for c in range(s):
                xb_ref[:, c * LANES:(c + 1) * LANES] = _line_block(xs_ref, c, tm, s).astype(BF16)
            acc_ref[...] = jnp.zeros(acc_ref.shape, F32)

        xb = xb_ref[...]
        a = jnp.dot(xb, w1_ref[...], preferred_element_type=F32)
        b = jnp.dot(xb, w3_ref[...], preferred_element_type=F32)
        hid = (a * _sigmoid(a) * b).astype(BF16)
        acc_ref[...] += jnp.dot(hid, w2_ref[...], preferred_element_type=F32)

        @pl.when(j == pl.num_programs(1) - 1)
        def _():
            _rows_to_lines(y_ref, acc_ref[...], tm, s)

    @pl.when((i >= nv_ref[0]) & (j == 0))
    def _():
        y_ref[...] = jnp.zeros(y_ref.shape, F32)


def _moe_ffn(xs, d, tile_expert, n_valid, w1, w3, w2, tm, tf):
    s = d // LANES
    p = xs.shape[0] // s
    f = w1.shape[2]
    assert p % tm == 0 and f % tf == 0
    nj = f // tf

    def row_map(i, j, te, nv):
        return (jnp.minimum(i, nv[0] - 1), 0)

    def jj(i, j, nv):
        return jnp.where(i < nv[0], j, nj - 1)

    grid_spec = pltpu.PrefetchScalarGridSpec(
        num_scalar_prefetch=2,
        grid=(p // tm, nj),
        in_specs=[pl.BlockSpec((tm * s, LANES), row_map),
                  pl.BlockSpec((None, d, tf), lambda i, j, te, nv: (te[i], 0, jj(i, j, nv))),
                  pl.BlockSpec((None, d, tf), lambda i, j, te, nv: (te[i], 0, jj(i, j, nv))),
                  pl.BlockSpec((None, tf, d), lambda i, j, te, nv: (te[i], jj(i, j, nv), 0))],
        out_specs=pl.BlockSpec((tm * s, LANES), lambda i, j, te, nv: (i, 0)),
        scratch_shapes=[pltpu.VMEM((tm, d), BF16), pltpu.VMEM((tm, d), F32)],
    )
    return pl.pallas_call(
        _moe_ffn_kernel,
        out_shape=jax.ShapeDtypeStruct((p * s, LANES), F32),
        grid_spec=grid_spec,
        compiler_params=_cparams(("arbitrary", "arbitrary"), 56),
        name="moe_ffn",
    )(tile_expert, n_valid, xs, w1, w3, w2)


def _combine_kernel(pos_ref, y_ref, g_ref, x_ref, lng_ref, lnb_ref, xo_ref,
                    buf0_ref, buf1_ref, z_ref, sem, *, alpha):
    tt, d = x_ref.shape
    s = d // LANES

    def copy(t, k):
        buf = buf0_ref if k == 0 else buf1_ref
        src = y_ref.at[pl.ds(pl.multiple_of(pos_ref[0, k, t] * s, s), s), :]
        return pltpu.make_async_copy(src, buf.at[pl.ds(pl.multiple_of(t * s, s), s), :], sem)

    _row_dma_loop(copy, tt)
    g0 = g_ref[:, 0:1]
    g1 = g_ref[:, 1:2]
    for c in range(s):
        cols = slice(c * LANES, (c + 1) * LANES)
        f = g0 * _line_block(buf0_ref, c, tt, s) + g1 * _line_block(buf1_ref, c, tt, s)
        z_ref[:, cols] = alpha * x_ref[:, cols] + f
    xo_ref[...] = _layer_norm(z_ref[...], lng_ref[...], lnb_ref[...])


def _combine(y, pos, gates, x, lng, lnb, alpha, tt):
    n, d = x.shape
    s = d // LANES
    return pl.pallas_call(
        functools.partial(_combine_kernel, alpha=alpha),
        out_shape=jax.ShapeDtypeStruct((n, d), F32),
        grid=(n // tt,),
        in_specs=[pl.BlockSpec((1, TOP_K, tt), lambda i: (i, 0, 0), memory_space=pltpu.SMEM),
                  pl.BlockSpec(memory_space=pl.ANY),
                  pl.BlockSpec((tt, LANES), lambda i: (i, 0)),
                  pl.BlockSpec((tt, d), lambda i: (i, 0)),
                  pl.BlockSpec((1, d), lambda i: (0, 0)),
                  pl.BlockSpec((1, d), lambda i: (0, 0))],
        out_specs=pl.BlockSpec((tt, d), lambda i: (i, 0)),
        scratch_shapes=[pltpu.VMEM((tt * s, LANES), F32),
                        pltpu.VMEM((tt * s, LANES), F32),
                        pltpu.VMEM((tt, d), F32),
                        pltpu.SemaphoreType.DMA],
        compiler_params=_cparams(("arbitrary",), 32),
        name="moe_combine",
    )(pos, y, gates, x, lng, lnb)


def _moe(x, wr, w1, w3, w2, lng, lnb, alpha, tm, tf, tr, tt):
    n, d = x.shape
    ints, flts, cnt = _router(x, wr, tr)
    e0, e1, r0, r1 = ints[:, 0], ints[:, 1], ints[:, 2], ints[:, 3]
    counts = cnt[0, :N_EXPERTS].astype(I32)

    tiles = (counts + tm - 1) // tm
    tile_end = jnp.cumsum(tiles)
    offs = (tile_end - tiles) * tm
    n_tiles = (TOP_K * n) // tm + N_EXPERTS
    n_valid = tile_end[-1:]
    tile_expert = jnp.sum(jnp.arange(n_tiles, dtype=I32)[:, None] >= tile_end[None, :], axis=1)
    tile_expert = jnp.minimum(tile_expert, N_EXPERTS - 1).astype(I32)
    tile_expert = jnp.where(jnp.arange(n_tiles) < n_valid[0], tile_expert,
                            tile_expert[jnp.maximum(n_valid[0] - 1, 0)])
    pos = jnp.stack([offs[e0] + r0, offs[e1] + r1], axis=0)
    pos = pos.reshape(TOP_K, n // tt, tt).transpose(1, 0, 2)

    xs = _dispatch(x, pos, n_tiles * tm, tt)
    y = _moe_ffn(xs, d, tile_expert, n_valid.astype(I32), w1, w3, w2, tm, tf)
    return _combine(y, pos, flts, x, lng[None, :], lnb[None, :], alpha, tt)


def _rope_tables(t_max):
    half = ATT_HEAD_DIM // 2
    pos = np.arange(t_max)
    pos_r = (pos // GRID_W).astype(np.float32)
    pos_c = (pos % GRID_W).astype(np.float32)
    inv = jnp.asarray(ROPE_THETA, F32) ** (-jnp.arange(0, half, 2, dtype=F32) / half)
    ang_r = jnp.asarray(pos_r)[:, None] * inv
    ang_c = jnp.asarray(pos_c)[:, None] * inv
    cr, sr, cc, sc = jnp.cos(ang_r), jnp.sin(ang_r), jnp.cos(ang_c), jnp.sin(ang_c)
    zero = jnp.zeros_like(sr)
    cos = jnp.concatenate([cr, cr, cc, cc], axis=1)
    sin_up = jnp.concatenate([-sr, zero, -sc, zero], axis=1)
    sin_dn = jnp.concatenate([zero, sr, zero, sc], axis=1)
    return cos, sin_up, sin_dn


def _split_w_in(w_in_l):
    sizes = (ATT_WIDTH, KV_WIDTH, KV_WIDTH, ML_WIDTH, ML_WIDTH, ML_WIDTH, ML_WIDTH, N_ML_GATES)
    o = np.cumsum((0,) + sizes)
    d = w_in_l.shape[0]
    wb = lax.optimization_barrier(w_in_l.astype(BF16))
    w_gate = jnp.concatenate([wb[:, o[7]:o[8]], jnp.zeros((d, LANES - N_ML_GATES), BF16)], axis=1)
    w_att = jnp.concatenate([wb[:, o[0]:o[3]], w_gate], axis=1)
    w_rest = jnp.concatenate([wb[:, o[3]:o[7]], wb[:, o[8]:]], axis=1)
    return w_att, w_rest


def kernel(x_prompt, x_sample, w_in, b_gates, q_gain, k_gain, w_att_br, w_ml_br, w_o, ln1_g, ln1_b,
           w1_d, w3_d, w2_d, router, e_w1, e_w3, e_w2, ln2_g, ln2_b):
    depth = w_in.shape[0]
    d = x_prompt.shape[-1]
    seqs = ((x_prompt.shape[0], x_prompt.shape[1]), (x_sample.shape[0], x_sample.shape[1]))
    n0 = seqs[0][0] * seqs[0][1]
    alpha = float((2 * depth) ** 0.25)

    x = jnp.concatenate([x_prompt.reshape(-1, d), x_sample.reshape(-1, d)], axis=0)
    xb = x.astype(BF16)
    cos, sin_up, sin_dn = _rope_tables(max(seqs[0][1], seqs[1][1]))

    for l in range(depth):
        w_att, w_rest = _split_w_in(w_in[l])
        bias = jnp.concatenate([b_gates[l], jnp.zeros((LANES - N_ML_GATES,), F32)])[None, :]
        gain = jnp.concatenate([jnp.broadcast_to(q_gain[l], (ATT_HEADS, ATT_HEAD_DIM)),
                                jnp.broadcast_to(k_gain[l], (ATT_KV_HEADS, ATT_HEAD_DIM)),
                                jnp.zeros((16 - ATT_HEADS - ATT_KV_HEADS, ATT_HEAD_DIM), F32)], axis=0)

        T = TILES
        qkv, gates = _att_proj(xb, w_att, bias, gain, cos, sin_up, sin_dn, seqs, tm=T["att_proj_tm"])
        rest = _matmul(xb, w_rest, BF16, tm=T["proj_tm"], tn=T["proj_tn"])

        a_out = jnp.concatenate(
            [_attention(qkv, 0, seqs[0][0], seqs[0][1], bq=min(T["attn_bq"], seqs[0][1]), tk=T["attn_tk"]),
             _attention(qkv, n0, seqs[1][0], seqs[1][1], bq=min(T["attn_bq"], seqs[1][1]), tk=T["attn_tk"])],
            axis=0)
        hf, hb = _mlstm(rest, rest[:, ML_WIDTH:2 * ML_WIDTH].T, gates[:, :N_ML_GATES].T, seqs)

        x, xb = _merge(a_out, hf, hb, rest, x, w_att_br[l].astype(BF16), w_ml_br[l].astype(BF16),
                       w_o[l].astype(BF16), ln1_g[l][None, :], ln1_b[l][None, :], alpha, tm=T["merge_tm"])

        j = l // 2
        if l % 2 == 0:
            nxt = []
            if l + 1 < depth:
                ne, _, nf = e_w1[j].shape
                nxt = [e_w1[j].reshape(ne * d, nf), e_w3[j].reshape(ne * d, nf), e_w2[j].reshape(ne * nf, d)]
            x, xb, cast = _ffn(xb, x, w1_d[j].astype(BF16), w3_d[j].astype(BF16), w2_d[j].astype(BF16),
                               ln2_g[l][None, :], ln2_b[l][None, :], alpha, tm=T["ffn_tm"], tf=T["ffn_tf"],
                               to_cast=nxt)
            expert_w = [c.reshape(e.shape) for c, e in zip(cast, (e_w1[j], e_w3[j], e_w2[j]))]
        else:
            wr = jnp.concatenate([router[j], jnp.zeros((d, LANES - N_EXPERTS), F32)], axis=1)
            x = _moe(x, wr, *expert_w, ln2_g[l], ln2_b[l], alpha, tm=T["moe_tm"], tf=T["moe_tf"],
                     tr=T["router_tr"], tt=T["moe_tt"])
            xb = x.astype(BF16)

    return (x[:n0].reshape(x_prompt.shape), x[n0:].reshape(x_sample.shape))
```

```python
import functools
import math

import numpy as np
import jax
import jax.numpy as jnp
from jax import lax
from jax.experimental import pallas as pl
from jax.experimental.pallas import tpu as pltpu

F32 = jnp.float32
BF16 = jnp.bfloat16
I32 = jnp.int32

GRID_W = 64
ATT_HEADS = 8
ATT_KV_HEADS = 2
ATT_HEAD_DIM = 128
ATT_GROUP = ATT_HEADS // ATT_KV_HEADS
ROPE_THETA = 10000.0
QK_EPS = 1e-6
ML_HEADS = 4
ML_HEAD_DIM = 256
ML_CHUNK = 128
ATT_WIDTH = ATT_HEADS * ATT_HEAD_DIM
KV_WIDTH = ATT_KV_HEADS * ATT_HEAD_DIM
ML_WIDTH = ML_HEADS * ML_HEAD_DIM
N_ML_GATES = 4 * ML_HEADS
N_EXPERTS = 8
TOP_K = 2
LN_EPS = 1e-5

LANES = 128
V7X_VMEM_BYTES = 64 * 2 ** 20
MIB = 2 ** 20

TILES = dict(att_proj_tm=512, proj_tm=1024, proj_tn=1024, attn_bq=1024, attn_tk=512, merge_tm=256,
             ffn_tm=512, ffn_tf=512, moe_tm=512, moe_tf=1024, router_tr=512, moe_tt=256)


def _cparams(semantics, vmem_mib):
    assert vmem_mib * MIB < V7X_VMEM_BYTES
    return pltpu.CompilerParams(dimension_semantics=semantics, vmem_limit_bytes=vmem_mib * MIB)


def _sigmoid(x):
    return 1.0 / (1.0 + jnp.exp(-x))


def _log_sigmoid(x):
    return jnp.minimum(x, 0.0) - jnp.log1p(jnp.exp(-jnp.abs(x)))


def _mean_last(z, n_axes):
    count = 1
    for ax in range(z.ndim - 1, z.ndim - 1 - n_axes, -1):
        count *= z.shape[ax]
        z = jnp.sum(z, axis=ax, keepdims=True)
    return z * (1.0 / count)


def _layer_norm(z, g, b, n_axes=1):
    mu = _mean_last(z, n_axes)
    zc = z - mu
    var = _mean_last(zc * zc, n_axes)
    return zc * lax.rsqrt(var + LN_EPS) * g + b


def _att_proj_kernel(x_ref, w_ref, bias_ref, gain_ref, cos_ref, sa_ref, sb_ref,
                     qkv_ref, gates_ref, *, n_heads, n_norm_heads):
    x = x_ref[...]
    cos = cos_ref[...]
    sa = sa_ref[...]
    sb = sb_ref[...]
    ones = jnp.ones((x.shape[0], LANES), qkv_ref.dtype)
    for h in range(n_heads):
        if h % 2 == 0:
            pair = jnp.dot(x, w_ref[:, h * LANES:(h + 2) * LANES], preferred_element_type=F32)
        a = pair[:, (h % 2) * LANES:(h % 2 + 1) * LANES]
        if h < n_norm_heads:
            a = a * lax.rsqrt(jnp.mean(a * a, axis=-1, keepdims=True) + QK_EPS) * gain_ref[h:h + 1, :]
            a = a * cos + pltpu.roll(a, 96, 1) * sa + pltpu.roll(a, 32, 1) * sb
            qkv_ref[:, h * LANES:(h + 1) * LANES] = a.astype(qkv_ref.dtype)
        else:
            c = n_norm_heads + 2 * (h - n_norm_heads)
            qkv_ref[:, c * LANES:(c + 1) * LANES] = a.astype(qkv_ref.dtype)
            qkv_ref[:, (c + 1) * LANES:(c + 2) * LANES] = ones
    gates_ref[...] = (jnp.dot(x, w_ref[:, n_heads * LANES:], preferred_element_type=F32)
                      + bias_ref[...])


def _att_proj(xb, w, bias, gain, cos, sa, sb, seqs, tm):
    n, d = xb.shape
    n_heads = (ATT_WIDTH + 2 * KV_WIDTH) // LANES
    ncol = n_heads * LANES + LANES
    assert w.shape == (d, ncol)
    n_out = (n_heads + ATT_KV_HEADS) * LANES

    (b0, t0), (b1, t1) = seqs
    assert t0 % tm == 0 and t1 % tm == 0
    nb0 = b0 * t0 // tm

    def pos_map(i):
        return (jnp.where(i < nb0, i % (t0 // tm), (i - nb0) % (t1 // tm)), 0)

    return pl.pallas_call(
        functools.partial(_att_proj_kernel, n_heads=n_heads, n_norm_heads=ATT_HEADS + ATT_KV_HEADS),
        out_shape=(jax.ShapeDtypeStruct((n, n_out), BF16),
                   jax.ShapeDtypeStruct((n, LANES), F32)),
        grid=(n // tm,),
        in_specs=[pl.BlockSpec((tm, d), lambda i: (i, 0)),
                  pl.BlockSpec((d, ncol), lambda i: (0, 0)),
                  pl.BlockSpec((1, LANES), lambda i: (0, 0)),
                  pl.BlockSpec((16, LANES), lambda i: (0, 0)),
                  pl.BlockSpec((tm, LANES), pos_map),
                  pl.BlockSpec((tm, LANES), pos_map),
                  pl.BlockSpec((tm, LANES), pos_map)],
        out_specs=(pl.BlockSpec((tm, n_out), lambda i: (i, 0)),
                   pl.BlockSpec((tm, LANES), lambda i: (i, 0))),
        compiler_params=_cparams(("parallel",), 48),
        name="att_proj",
    )(xb, w, bias, gain, cos, sa, sb)


def _matmul_kernel(x_ref, w_ref, o_ref):
    o_ref[...] = jnp.dot(x_ref[...], w_ref[...], preferred_element_type=F32).astype(o_ref.dtype)


def _matmul(xb, w, out_dtype, tm, tn):
    n, d = xb.shape
    ncol = w.shape[1]
    assert n % tm == 0 and ncol % tn == 0
    return pl.pallas_call(
        _matmul_kernel,
        out_shape=jax.ShapeDtypeStruct((n, ncol), out_dtype),
        grid=(ncol // tn, n // tm),
        in_specs=[pl.BlockSpec((tm, d), lambda j, i: (i, 0)),
                  pl.BlockSpec((d, tn), lambda j, i: (0, j))],
        out_specs=pl.BlockSpec((tm, tn), lambda j, i: (i, j)),
        compiler_params=_cparams(("parallel", "parallel"), 48),
        name="matmul",
    )(xb, w)


def _attn_kernel(q_ref, k_ref, v_ref, o_ref, m_ref, acc_ref, *, tk, scale):
    bq = q_ref.shape[0]
    t = k_ref.shape[0]
    c = scale * math.log2(math.e)
    m_ref[...] = jnp.full(m_ref.shape, -jnp.inf, F32)
    acc_ref[...] = jnp.zeros(acc_ref.shape, F32)

    def body(ci, carry):
        start = pl.multiple_of(ci * tk, tk)
        k = k_ref[pl.ds(start, tk), :]
        v = v_ref[pl.ds(start, tk), :]
        for h in range(ATT_GROUP):
            r = slice(h * bq, (h + 1) * bq)
            s = lax.dot_general(q_ref[:, h * LANES:(h + 1) * LANES], k, (((1,), (1,)), ((), ())),
                                preferred_element_type=F32)
            m_prev = m_ref[r, :]
            m_new = jnp.maximum(m_prev, jnp.max(s, axis=1, keepdims=True))
            p = jnp.exp2((s - jnp.tile(m_new, (1, tk // LANES))) * c)
            alpha = jnp.exp2((m_prev - m_new) * c)
            pv = jnp.dot(p.astype(BF16), v, preferred_element_type=F32)
            acc_ref[r, :] = jnp.tile(alpha, (1, 2)) * acc_ref[r, :] + pv
            m_ref[r, :] = m_new
        return carry

    lax.fori_loop(0, t // tk, body, 0, unroll=2)
    for h in range(ATT_GROUP):
        r = slice(h * bq, (h + 1) * bq)
        o_ref[:, h * LANES:(h + 1) * LANES] = (acc_ref[r, :LANES] / acc_ref[r, LANES:]).astype(o_ref.dtype)


def _attention(qkv, tok0, nseq, t, bq, tk):
    assert t % bq == 0 and t % tk == 0 and tok0 % t == 0
    qb0 = tok0 // bq
    sb0 = tok0 // t
    nq = t // bq
    gw = ATT_GROUP * LANES
    v_blk0 = (ATT_HEADS + ATT_KV_HEADS) // 2
    return pl.pallas_call(
        functools.partial(_attn_kernel, tk=tk, scale=ATT_HEAD_DIM ** -0.5),
        out_shape=jax.ShapeDtypeStruct((nseq * t, ATT_WIDTH), BF16),
        grid=(nseq, ATT_KV_HEADS, nq),
        in_specs=[pl.BlockSpec((bq, gw), lambda b, g, i: (qb0 + b * nq + i, g)),
                  pl.BlockSpec((t, LANES), lambda b, g, i: (sb0 + b, ATT_HEADS + g)),
                  pl.BlockSpec((t, 2 * LANES), lambda b, g, i: (sb0 + b, v_blk0 + g))],
        out_specs=pl.BlockSpec((bq, gw), lambda b, g, i: (b * nq + i, g)),
        scratch_shapes=[pltpu.VMEM((ATT_GROUP * bq, LANES), F32),
                        pltpu.VMEM((ATT_GROUP * bq, 2 * LANES), F32)],
        compiler_params=_cparams(("parallel", "parallel", "parallel"), 40),
        name="attention",
    )(qkv, qkv, qkv)


def _mlstm_chain(d, hd, q, k, kt, v, gt, h_ref, c_ref, n_ref, m_ref):
    L, dk = q.shape
    ci = d * ML_HEADS + hd
    cf = 2 * ML_HEADS + ci
    ig_row = gt[ci:ci + 1, :]
    logf_row = _log_sigmoid(gt[cf:cf + 1, :])

    row = lax.broadcasted_iota(I32, (L, L), 0)
    col = lax.broadcasted_iota(I32, (L, L), 1)
    mask = (col <= row) if d == 0 else (col >= row)
    b_col = jnp.sum(jnp.where(mask, logf_row, 0.0), axis=1, keepdims=True)
    b_row = jnp.sum(jnp.where(row == col, b_col, 0.0), axis=0, keepdims=True)
    gtot = jnp.sum(logf_row, axis=1, keepdims=True)

    m_prev = m_ref[...]
    scale = dk ** -0.5
    assert math.log2(dk) % 2 == 0

    a_row = gtot - b_row + ig_row
    m_loc = jnp.max(a_row, axis=1, keepdims=True)
    w_row = jnp.exp(a_row - m_loc) * scale
    c_loc = jnp.dot((kt.astype(F32) * w_row).astype(BF16), v, preferred_element_type=F32)
    n_loc = jnp.dot(jnp.broadcast_to(w_row, (8, L)).astype(BF16), k, preferred_element_type=F32)[0:1, :]

    dmat = jnp.where(mask, b_col - b_row + ig_row, -jnp.inf)
    inter = b_col + m_prev
    m_j = jnp.maximum(jnp.max(dmat, axis=1, keepdims=True), inter)
    qk = jnp.dot(q, kt, preferred_element_type=F32)
    s = qk * (jnp.exp(dmat - m_j) * scale)
    s_int = jnp.exp(inter - m_j)
    num = (jnp.dot(s.astype(BF16), v, preferred_element_type=F32)
           + s_int * jnp.dot(q, c_ref[...].astype(BF16), preferred_element_type=F32))
    den = (jnp.sum(s, axis=1, keepdims=True)
           + s_int * jnp.sum(q.astype(F32) * n_ref[...], axis=1, keepdims=True))
    h_ref[...] = num / jnp.maximum(jnp.abs(den), jnp.exp(-m_j))

    m_new = jnp.maximum(gtot + m_prev, m_loc)
    s_old = jnp.exp(gtot + m_prev - m_new)
    s_new = jnp.exp(m_loc - m_new)
    c_ref[...] = s_old * c_ref[...] + s_new * c_loc
    n_ref[...] = s_old * n_ref[...] + s_new * n_loc
    m_ref[...] = m_new


def _mlstm_kernel(first_ref, qf_ref, kf_ref, vf_ref, ktf_ref, gtf_ref, qb_ref, kb_ref, vb_ref, ktb_ref,
                  gtb_ref, hf_ref, hb_ref, c_ref, n_ref, m_ref):
    i = pl.program_id(0)
    dirs = ((qf_ref, kf_ref, vf_ref, ktf_ref, gtf_ref, hf_ref),
            (qb_ref, kb_ref, vb_ref, ktb_ref, gtb_ref, hb_ref))
    for d in range(2):
        @pl.when(first_ref[d, i] == 1)
        def _():
            for hd in range(ML_HEADS):
                s = d * ML_HEADS + hd
                c_ref[s] = jnp.zeros(c_ref.shape[1:], F32)
                n_ref[s] = jnp.zeros(n_ref.shape[1:], F32)
                m_ref[s] = jnp.zeros(m_ref.shape[1:], F32)

    for d, (q_ref, k_ref, v_ref, kt_ref, gt_ref, h_ref) in enumerate(dirs):
        gt = gt_ref[...]
        for hd in range(ML_HEADS):
            s = d * ML_HEADS + hd
            cols = slice(hd * ML_HEAD_DIM, (hd + 1) * ML_HEAD_DIM)
            _mlstm_chain(d, hd, q_ref[:, cols], k_ref[:, cols], kt_ref[cols, :], v_ref[:, cols], gt,
                         h_ref.at[:, cols], c_ref.at[s], n_ref.at[s], m_ref.at[s])


def _mlstm(rest, k_t, gates_t, seqs):
    n = rest.shape[0]
    L = ML_CHUNK
    nc = n // L
    first_f = np.zeros((nc,), np.int32)
    last_f = np.zeros((nc,), np.int32)
    tok = 0
    for (b, t) in seqs:
        for _ in range(b):
            first_f[tok // L] = 1
            last_f[(tok + t) // L - 1] = 1
            tok += t
    first = np.stack([first_f, last_f[::-1]]).astype(np.int32)
    W = ML_WIDTH
    nstate = 2 * ML_HEADS

    def fwd(c):
        return lambda i, fst: (i, c)

    def bwd(c):
        return lambda i, fst: (nc - 1 - i, c)

    def specs(m, mt):
        return [pl.BlockSpec((L, W), m(0)), pl.BlockSpec((L, W), m(1)), pl.BlockSpec((L, W), m(2)),
                pl.BlockSpec((W, L), mt), pl.BlockSpec((N_ML_GATES, L), mt)]

    grid_spec = pltpu.PrefetchScalarGridSpec(
        num_scalar_prefetch=1,
        grid=(nc,),
        in_specs=(specs(fwd, lambda i, fst: (0, i)) + specs(bwd, lambda i, fst: (0, nc - 1 - i))),
        out_specs=(pl.BlockSpec((L, W), fwd(0)), pl.BlockSpec((L, W), bwd(0))),
        scratch_shapes=[pltpu.VMEM((nstate, ML_HEAD_DIM, ML_HEAD_DIM), F32),
                        pltpu.VMEM((nstate, 1, ML_HEAD_DIM), F32),
                        pltpu.VMEM((nstate, 1, 1), F32)],
    )
    return pl.pallas_call(
        _mlstm_kernel,
        out_shape=(jax.ShapeDtypeStruct((n, W), F32), jax.ShapeDtypeStruct((n, W), F32)),
        grid_spec=grid_spec,
        compiler_params=_cparams(("arbitrary",), 32),
        name="mlstm",
    )(jnp.asarray(first), rest, rest, rest, k_t, gates_t, rest, rest, rest, k_t, gates_t)


def _merge_kernel(a_ref, hf_ref, hb_ref, mo_ref, ga_ref, gm_ref, x_ref, wa_ref, wm_ref, wo_ref,
                  lng_ref, lnb_ref, xo_ref, xob_ref, *, alpha):
    m_out = (_sigmoid(mo_ref[...].astype(F32)) * (hf_ref[...] + hb_ref[...])).astype(BF16)
    pa = jnp.dot(a_ref[...], wa_ref[...], preferred_element_type=F32)
    pm = jnp.dot(m_out, wm_ref[...], preferred_element_type=F32)
    merged = _sigmoid(ga_ref[...].astype(F32)) * pa + _sigmoid(gm_ref[...].astype(F32)) * pm
    y = jnp.dot(merged.astype(BF16), wo_ref[...], preferred_element_type=F32)
    z = _layer_norm(alpha * x_ref[...] + y, lng_ref[...], lnb_ref[...])
    xo_ref[...] = z
    xob_ref[...] = z.astype(BF16)


def _merge(a_out, hf, hb, rest, x, wa, wm, wo, lng, lnb, alpha, tm):
    n, d = x.shape
    assert n % tm == 0 and d == 2 * ML_WIDTH and ATT_WIDTH == ML_WIDTH
    const = dict(pipeline_mode=pl.Buffered(1))
    return pl.pallas_call(
        functools.partial(_merge_kernel, alpha=alpha),
        out_shape=(jax.ShapeDtypeStruct((n, d), F32), jax.ShapeDtypeStruct((n, d), BF16)),
        grid=(n // tm,),
        in_specs=[pl.BlockSpec((tm, ATT_WIDTH), lambda i: (i, 0)),
                  pl.BlockSpec((tm, ML_WIDTH), lambda i: (i, 0)),
                  pl.BlockSpec((tm, ML_WIDTH), lambda i: (i, 0)),
                  pl.BlockSpec((tm, ML_WIDTH), lambda i: (i, 3)),
                  pl.BlockSpec((tm, d), lambda i: (i, 2)),
                  pl.BlockSpec((tm, d), lambda i: (i, 3)),
                  pl.BlockSpec((tm, d), lambda i: (i, 0)),
                  pl.BlockSpec((ATT_WIDTH, d), lambda i: (0, 0), **const),
                  pl.BlockSpec((ML_WIDTH, d), lambda i: (0, 0), **const),
                  pl.BlockSpec((d, d), lambda i: (0, 0), **const),
                  pl.BlockSpec((1, d), lambda i: (0, 0)),
                  pl.BlockSpec((1, d), lambda i: (0, 0))],
        out_specs=(pl.BlockSpec((tm, d), lambda i: (i, 0)),
                   pl.BlockSpec((tm, d), lambda i: (i, 0))),
        compiler_params=_cparams(("parallel",), 56),
        name="merge",
    )(a_out, hf, hb, rest, rest, rest, x, wa, wm, wo, lng, lnb)


def _ffn_kernel(xb_ref, w1_ref, w3_ref, w2_ref, x_ref, lng_ref, lnb_ref, *refs, alpha, n_cast):
    cast_in = refs[:n_cast]
    xo_ref, xob_ref = refs[n_cast:n_cast + 2]
    cast_out = refs[n_cast + 2:2 * n_cast + 2]
    acc_ref = refs[-1]
    j = pl.program_id(1)

    for src, dst in zip(cast_in, cast_out):
        dst[...] = src[...].astype(BF16)

    @pl.when(j == 0)
    def _():
        acc_ref[...] = jnp.zeros(acc_ref.shape, F32)

    xb = xb_ref[...]
    a = jnp.dot(xb, w1_ref[...], preferred_element_type=F32)
    b = jnp.dot(xb, w3_ref[...], preferred_element_type=F32)
    hid = (a * _sigmoid(a) * b).astype(BF16)
    acc_ref[...] += jnp.dot(hid, w2_ref[...], preferred_element_type=F32)

    @pl.when(j == pl.num_programs(1) - 1)
    def _():
        z = _layer_norm(alpha * x_ref[...] + acc_ref[...], lng_ref[...], lnb_ref[...])
        xo_ref[...] = z
        xob_ref[...] = z.astype(BF16)


def _cast_block_rows(rows, steps):
    rb = 16 * pl.cdiv(pl.cdiv(rows, steps), 16)
    while rows % rb:
        rb += 16
    return rb


def _ffn(xb, x, w1, w3, w2, lng, lnb, alpha, tm, tf, to_cast=()):
    n, d = x.shape
    f = w1.shape[1]
    assert n % tm == 0 and f % tf == 0
    nj = f // tf
    steps = (n // tm) * nj
    cast_specs = []
    for a in to_cast:
        rb = _cast_block_rows(a.shape[0], steps)
        last = a.shape[0] // rb - 1
        cast_specs.append(pl.BlockSpec((rb, a.shape[1]),
                                       lambda i, j, last=last: (jnp.minimum(i * nj + j, last), 0)))
    outs = pl.pallas_call(
        functools.partial(_ffn_kernel, alpha=alpha, n_cast=len(to_cast)),
        out_shape=(jax.ShapeDtypeStruct((n, d), F32), jax.ShapeDtypeStruct((n, d), BF16),
                   *[jax.ShapeDtypeStruct(a.shape, BF16) for a in to_cast]),
        grid=(n // tm, nj),
        in_specs=[pl.BlockSpec((tm, d), lambda i, j: (i, 0)),
                  pl.BlockSpec((d, tf), lambda i, j: (0, j)),
                  pl.BlockSpec((d, tf), lambda i, j: (0, j)),
                  pl.BlockSpec((tf, d), lambda i, j: (j, 0)),
                  pl.BlockSpec((tm, d), lambda i, j: (i, 0)),
                  pl.BlockSpec((1, d), lambda i, j: (0, 0)),
                  pl.BlockSpec((1, d), lambda i, j: (0, 0)),
                  *cast_specs],
        out_specs=(pl.BlockSpec((tm, d), lambda i, j: (i, 0)),
                   pl.BlockSpec((tm, d), lambda i, j: (i, 0)),
                   *cast_specs),
        scratch_shapes=[pltpu.VMEM((tm, d), F32)],
        compiler_params=_cparams(("arbitrary", "arbitrary"), 56),
        name="ffn_dense",
    )(xb, w1, w3, w2, x, lng, lnb, *to_cast)
    return outs[0], outs[1], outs[2:]


def _router_kernel(x_ref, wr_ref, tri_ref, ints_ref, flts_ref, cnt_ref, base_ref):
    i = pl.program_id(0)

    @pl.when(i == 0)
    def _():
        base_ref[...] = jnp.zeros(base_ref.shape, F32)

    logits = jnp.dot(x_ref[...], wr_ref[...], preferred_element_type=F32,
                     precision=lax.Precision.HIGHEST)
    lane = lax.broadcasted_iota(I32, logits.shape, 1)
    lanef = lane.astype(F32)
    logits = jnp.where(lane < N_EXPERTS, logits, -jnp.inf)
    v0 = jnp.max(logits, axis=1, keepdims=True)
    i0 = jnp.min(jnp.where(logits == v0, lanef, float(LANES)), axis=1, keepdims=True)
    rest = jnp.where(lanef == i0, -jnp.inf, logits)
    v1 = jnp.max(rest, axis=1, keepdims=True)
    i1 = jnp.min(jnp.where(rest == v1, lanef, float(LANES)), axis=1, keepdims=True)
    e1 = jnp.exp(v1 - v0)
    g0 = 1.0 / (1.0 + e1)
    g1 = e1 / (1.0 + e1)

    sel0 = lanef == i0
    sel1 = lanef == i1
    onehot = jnp.where(sel0 | sel1, 1.0, 0.0)
    before = jnp.dot(tri_ref[...], onehot.astype(BF16), preferred_element_type=F32) + base_ref[...]
    r0 = jnp.sum(jnp.where(sel0, before, 0.0), axis=1, keepdims=True)
    r1 = jnp.sum(jnp.where(sel1, before, 0.0), axis=1, keepdims=True)
    base_ref[...] += jnp.sum(onehot, axis=0, keepdims=True)
    cnt_ref[...] = jnp.broadcast_to(base_ref[...], cnt_ref.shape)

    ints = jnp.where(lane == 0, i0, jnp.where(lane == 1, i1, jnp.where(lane == 2, r0,
                     jnp.where(lane == 3, r1, 0.0))))
    ints_ref[...] = ints.astype(I32)
    flts_ref[...] = jnp.where(lane == 0, g0, jnp.where(lane == 1, g1, 0.0))


def _router(x, wr, tr):
    n, d = x.shape
    assert n % tr == 0
    tri = jnp.asarray(np.tril(np.ones((tr, tr), np.float32), -1), BF16)
    return pl.pallas_call(
        _router_kernel,
        out_shape=(jax.ShapeDtypeStruct((n, LANES), I32),
                   jax.ShapeDtypeStruct((n, LANES), F32),
                   jax.ShapeDtypeStruct((8, LANES), F32)),
        grid=(n // tr,),
        in_specs=[pl.BlockSpec((tr, d), lambda i: (i, 0)),
                  pl.BlockSpec((d, LANES), lambda i: (0, 0)),
                  pl.BlockSpec((tr, tr), lambda i: (0, 0))],
        out_specs=(pl.BlockSpec((tr, LANES), lambda i: (i, 0)),
                   pl.BlockSpec((tr, LANES), lambda i: (i, 0)),
                   pl.BlockSpec((8, LANES), lambda i: (0, 0))),
        scratch_shapes=[pltpu.VMEM((1, LANES), F32)],
        compiler_params=_cparams(("arbitrary",), 32),
        name="moe_router",
    )(x, wr, tri)


def _row_dma_loop(make_copy, tt):
    def issue(t, carry):
        for k in range(TOP_K):
            make_copy(t, k).start()
        return carry

    def drain(t, carry):
        for k in range(TOP_K):
            make_copy(t, k).wait()
        return carry

    lax.fori_loop(0, tt, issue, 0)
    lax.fori_loop(0, tt, drain, 0)


def _dispatch_kernel(pos_ref, x_ref, xs_in_ref, xs_ref, sem):
    del xs_in_ref

    def copy(t, k):
        return pltpu.make_async_copy(x_ref.at[pl.ds(t, 1), :],
                                     xs_ref.at[pl.ds(pos_ref[0, k, t], 1), :], sem)

    _row_dma_loop(copy, x_ref.shape[0])


def _dispatch(x, pos, n_rows, tt):
    n, d = x.shape
    xs0 = jnp.zeros((n_rows, d), x.dtype)
    return pl.pallas_call(
        _dispatch_kernel,
        out_shape=jax.ShapeDtypeStruct(xs0.shape, xs0.dtype),
        grid=(n // tt,),
        in_specs=[pl.BlockSpec((1, TOP_K, tt), lambda i: (i, 0, 0), memory_space=pltpu.SMEM),
                  pl.BlockSpec((tt, d), lambda i: (i, 0)),
                  pl.BlockSpec(memory_space=pl.ANY)],
        out_specs=pl.BlockSpec(memory_space=pl.ANY),
        scratch_shapes=[pltpu.SemaphoreType.DMA],
        input_output_aliases={2: 0},
        compiler_params=_cparams(("arbitrary",), 32),
        name="moe_dispatch",
    )(pos, x, xs0)


def _moe_ffn_kernel(te_ref, nv_ref, xs_ref, w1_ref, w3_ref, w2_ref, y_ref, xb_ref):
    i = pl.program_id(0)
    j = pl.program_id(1)

    @pl.when(i < nv_ref[0])
    def _():
        @pl.when(j == 0)
        def _():
            xb_ref[...] = xs_ref[...].astype(BF16)
            y_ref[...] = jnp.zeros(y_ref.shape, F32)

        xb = xb_ref[...]
        a = jnp.dot(xb, w1_ref[...], preferred_element_type=F32)
        b = jnp.dot(xb, w3_ref[...], preferred_element_type=F32)
        hid = (a * _sigmoid(a) * b).astype(BF16)
        y_ref[...] += jnp.dot(hid, w2_ref[...], preferred_element_type=F32)

    @pl.when((i >= nv_ref[0]) & (j == 0))
    def _():
        y_ref[...] = jnp.zeros(y_ref.shape, F32)


def _moe_ffn(xs, tile_expert, n_valid, w1, w3, w2, tm, tf):
    p, d = xs.shape
    f = w1.shape[2]
    assert p % tm == 0 and f % tf == 0
    nj = f // tf

    def row_map(i, j, te, nv):
        return (jnp.minimum(i, nv[0] - 1), 0)

    def jj(i, j, nv):
        return jnp.where(i < nv[0], j, nj - 1)

    grid_spec = pltpu.PrefetchScalarGridSpec(
        num_scalar_prefetch=2,
        grid=(p // tm, nj),
        in_specs=[pl.BlockSpec((tm, d), row_map),
                  pl.BlockSpec((None, d, tf), lambda i, j, te, nv: (te[i], 0, jj(i, j, nv))),
                  pl.BlockSpec((None, d, tf), lambda i, j, te, nv: (te[i], 0, jj(i, j, nv))),
                  pl.BlockSpec((None, tf, d), lambda i, j, te, nv: (te[i], jj(i, j, nv), 0))],
        out_specs=pl.BlockSpec((tm, d), lambda i, j, te, nv: (i, 0)),
        scratch_shapes=[pltpu.VMEM((tm, d), BF16)],
    )
    return pl.pallas_call(
        _moe_ffn_kernel,
        out_shape=jax.ShapeDtypeStruct((p, d), F32),
        grid_spec=grid_spec,
        compiler_params=_cparams(("arbitrary", "arbitrary"), 56),
        name="moe_ffn",
    )(tile_expert, n_valid, xs, w1, w3, w2)


def _combine_kernel(pos_ref, y_ref, g_ref, x_ref, lng_ref, lnb_ref, *refs, alpha, split_block):
    out_refs = refs[:-3]
    buf0_ref, buf1_ref, sem = refs[-3:]

    def copy(t, k):
        buf = buf0_ref if k == 0 else buf1_ref
        return pltpu.make_async_copy(y_ref.at[pl.ds(pos_ref[0, k, t], 1), :], buf.at[pl.ds(t, 1), :], sem)

    _row_dma_loop(copy, x_ref.shape[0])
    f = g_ref[:, 0:1] * buf0_ref[...] + g_ref[:, 1:2] * buf1_ref[...]
    z = _layer_norm(alpha * x_ref[...] + f, lng_ref[...], lnb_ref[...])
    if split_block is None:
        out_refs[0][...] = z
    else:
        @pl.when(pl.program_id(0) < split_block)
        def _():
            out_refs[0][...] = z

        @pl.when(pl.program_id(0) >= split_block)
        def _():
            out_refs[1][...] = z


def _combine(y, pos, gates, x, lng, lnb, alpha, tt, split=None):
    n, d = x.shape
    if split is None:
        sb = None
        out_shape = jax.ShapeDtypeStruct((n, d), F32)
        out_specs = pl.BlockSpec((tt, d), lambda i: (i, 0))
    else:
        assert split % tt == 0 and 0 < split < n
        sb = split // tt
        out_shape = (jax.ShapeDtypeStruct((split, d), F32), jax.ShapeDtypeStruct((n - split, d), F32))
        out_specs = (pl.BlockSpec((tt, d), lambda i: (jnp.minimum(i, sb - 1), 0)),
                     pl.BlockSpec((tt, d), lambda i: (jnp.maximum(i - sb, 0), 0)))
    return pl.pallas_call(
        functools.partial(_combine_kernel, alpha=alpha, split_block=sb),
        out_shape=out_shape,
        grid=(n // tt,),
        in_specs=[pl.BlockSpec((1, TOP_K, tt), lambda i: (i, 0, 0), memory_space=pltpu.SMEM),
                  pl.BlockSpec(memory_space=pl.ANY),
                  pl.BlockSpec((tt, LANES), lambda i: (i, 0)),
                  pl.BlockSpec((tt, d), lambda i: (i, 0)),
                  pl.BlockSpec((1, d), lambda i: (0, 0)),
                  pl.BlockSpec((1, d), lambda i: (0, 0))],
        out_specs=out_specs,
        scratch_shapes=[pltpu.VMEM((tt, d), F32), pltpu.VMEM((tt, d), F32), pltpu.SemaphoreType.DMA],
        compiler_params=_cparams(("arbitrary",), 32),
        name="moe_combine",
    )(pos, y, gates, x, lng, lnb)


def _moe(x, wr, w1, w3, w2, lng, lnb, alpha, tm, tf, tr, tt, split=None):
    n, d = x.shape
    ints, flts, cnt = _router(x, wr, tr)
    e0, e1, r0, r1 = ints[:, 0], ints[:, 1], ints[:, 2], ints[:, 3]
    counts = cnt[0, :N_EXPERTS].astype(I32)

    tiles = (counts + tm - 1) // tm
    tile_end = jnp.cumsum(tiles)
    offs = (tile_end - tiles) * tm
    n_tiles = (TOP_K * n) // tm + N_EXPERTS
    n_valid = tile_end[-1:]
    tile_expert = jnp.sum(jnp.arange(n_tiles, dtype=I32)[:, None] >= tile_end[None, :], axis=1)
    tile_expert = jnp.minimum(tile_expert, N_EXPERTS - 1).astype(I32)
    tile_expert = jnp.where(jnp.arange(n_tiles) < n_valid[0], tile_expert,
                            tile_expert[jnp.maximum(n_valid[0] - 1, 0)])
    pos = jnp.stack([offs[e0] + r0, offs[e1] + r1], axis=0)
    pos = pos.reshape(TOP_K, n // tt, tt).transpose(1, 0, 2)

    xs = _dispatch(x, pos, n_tiles * tm, tt)
    y = _moe_ffn(xs, tile_expert, n_valid.astype(I32), w1, w3, w2, tm, tf)
    return _combine(y, pos, flts, x, lng[None, :], lnb[None, :], alpha, tt, split=split)


def _rope_tables(t_max):
    half = ATT_HEAD_DIM // 2
    pos = np.arange(t_max)
    pos_r = (pos // GRID_W).astype(np.float32)
    pos_c = (pos % GRID_W).astype(np.float32)
    inv = jnp.asarray(ROPE_THETA, F32) ** (-jnp.arange(0, half, 2, dtype=F32) / half)
    ang_r = jnp.asarray(pos_r)[:, None] * inv
    ang_c = jnp.asarray(pos_c)[:, None] * inv
    cr, sr, cc, sc = jnp.cos(ang_r), jnp.sin(ang_r), jnp.cos(ang_c), jnp.sin(ang_c)
    zero = jnp.zeros_like(sr)
    cos = jnp.concatenate([cr, cr, cc, cc], axis=1)
    sin_up = jnp.concatenate([-sr, zero, -sc, zero], axis=1)
    sin_dn = jnp.concatenate([zero, sr, zero, sc], axis=1)
    return cos, sin_up, sin_dn


def _split_w_in(w_in_l):
    sizes = (ATT_WIDTH, KV_WIDTH, KV_WIDTH, ML_WIDTH, ML_WIDTH, ML_WIDTH, ML_WIDTH, N_ML_GATES)
    o = np.cumsum((0,) + sizes)
    d = w_in_l.shape[0]
    wb = lax.optimization_barrier(w_in_l.astype(BF16))
    w_gate = jnp.concatenate([wb[:, o[7]:o[8]], jnp.zeros((d, LANES - N_ML_GATES), BF16)], axis=1)
    w_att = jnp.concatenate([wb[:, o[0]:o[3]], w_gate], axis=1)
    w_rest = jnp.concatenate([wb[:, o[3]:o[7]], wb[:, o[8]:]], axis=1)
    return w_att, w_rest


def kernel(x_prompt, x_sample, w_in, b_gates, q_gain, k_gain, w_att_br, w_ml_br, w_o, ln1_g, ln1_b,
           w1_d, w3_d, w2_d, router, e_w1, e_w3, e_w2, ln2_g, ln2_b):
    depth = w_in.shape[0]
    d = x_prompt.shape[-1]
    seqs = ((x_prompt.shape[0], x_prompt.shape[1]), (x_sample.shape[0], x_sample.shape[1]))
    n0 = seqs[0][0] * seqs[0][1]
    alpha = float((2 * depth) ** 0.25)

    x = jnp.concatenate([x_prompt.reshape(-1, d), x_sample.reshape(-1, d)], axis=0)
    xb = x.astype(BF16)
    cos, sin_up, sin_dn = _rope_tables(max(seqs[0][1], seqs[1][1]))

    for l in range(depth):
        w_att, w_rest = _split_w_in(w_in[l])
        bias = jnp.concatenate([b_gates[l], jnp.zeros((LANES - N_ML_GATES,), F32)])[None, :]
        gain = jnp.concatenate([jnp.broadcast_to(q_gain[l], (ATT_HEADS, ATT_HEAD_DIM)),
                                jnp.broadcast_to(k_gain[l], (ATT_KV_HEADS, ATT_HEAD_DIM)),
                                jnp.zeros((16 - ATT_HEADS - ATT_KV_HEADS, ATT_HEAD_DIM), F32)], axis=0)

        T = TILES
        qkv, gates = _att_proj(xb, w_att, bias, gain, cos, sin_up, sin_dn, seqs, tm=T["att_proj_tm"])
        rest = _matmul(xb, w_rest, BF16, tm=T["proj_tm"], tn=T["proj_tn"])

        a_out = jnp.concatenate(
            [_attention(qkv, 0, seqs[0][0], seqs[0][1], bq=min(T["attn_bq"], seqs[0][1]), tk=T["attn_tk"]),
             _attention(qkv, n0, seqs[1][0], seqs[1][1], bq=min(T["attn_bq"], seqs[1][1]), tk=T["attn_tk"])],
            axis=0)
        hf, hb = _mlstm(rest, rest[:, ML_WIDTH:2 * ML_WIDTH].T, gates[:, :N_ML_GATES].T, seqs)

        x, xb = _merge(a_out, hf, hb, rest, x, w_att_br[l].astype(BF16), w_ml_br[l].astype(BF16),
                       w_o[l].astype(BF16), ln1_g[l][None, :], ln1_b[l][None, :], alpha, tm=T["merge_tm"])

        j = l // 2
        if l % 2 == 0:
            nxt = (e_w1[j], e_w3[j], e_w2[j]) if l + 1 < depth else ()
            x, xb, cast = _ffn(xb, x, w1_d[j].astype(BF16), w3_d[j].astype(BF16), w2_d[j].astype(BF16),
                               ln2_g[l][None, :], ln2_b[l][None, :], alpha, tm=T["ffn_tm"], tf=T["ffn_tf"],
                               to_cast=[e.reshape(-1, e.shape[-1]) for e in nxt])
            expert_w = [c.reshape(e.shape) for c, e in zip(cast, nxt)]
        else:
            wr = jnp.concatenate([router[j], jnp.zeros((d, LANES - N_EXPERTS), F32)], axis=1)
            last = l == depth - 1
            x = _moe(x, wr, *expert_w, ln2_g[l], ln2_b[l], alpha, tm=T["moe_tm"], tf=T["moe_tf"],
                     tr=T["router_tr"], tt=T["moe_tt"], split=n0 if last else None)
            if last:
                return (x[0].reshape(x_prompt.shape), x[1].reshape(x_sample.shape))
            xb = x.astype(BF16)

    return (x[:n0].reshape(x_prompt.shape), x[n0:].reshape(x_sample.shape))
```

```python
import functools
import math

import numpy as np
import jax
import jax.numpy as jnp
from jax import lax
from jax.experimental import pallas as pl
from jax.experimental.pallas import tpu as pltpu

F32 = jnp.float32
BF16 = jnp.bfloat16
I32 = jnp.int32

GRID_W = 64
ATT_HEADS = 8
ATT_KV_HEADS = 2
ATT_HEAD_DIM = 128
ATT_GROUP = ATT_HEADS // ATT_KV_HEADS
ROPE_THETA = 10000.0
QK_EPS = 1e-6
ML_HEADS = 4
ML_HEAD_DIM = 256
ML_CHUNK = 128
ML_INTERLEAVE = 2
ATT_WIDTH = ATT_HEADS * ATT_HEAD_DIM
KV_WIDTH = ATT_KV_HEADS * ATT_HEAD_DIM
ML_WIDTH = ML_HEADS * ML_HEAD_DIM
N_ML_GATES = 4 * ML_HEADS
N_EXPERTS = 8
TOP_K = 2
LN_EPS = 1e-5

LANES = 128
V7X_VMEM_BYTES = 64 * 2 ** 20
MIB = 2 ** 20

TILES = dict(att_proj_tm=512, proj_tm=1024, proj_tn=1024, attn_bq=1024, attn_tk=512, merge_tm=256,
             ffn_tm=512, ffn_tf=512, moe_tm=512, moe_tf=1024, router_tr=512, moe_tt=256, mlstm_chunks=2)


def _cparams(semantics, vmem_mib):
    assert vmem_mib * MIB < V7X_VMEM_BYTES
    return pltpu.CompilerParams(dimension_semantics=semantics, vmem_limit_bytes=vmem_mib * MIB)


def _sigmoid(x):
    return 1.0 / (1.0 + jnp.exp(-x))


def _log_sigmoid(x):
    return jnp.minimum(x, 0.0) - jnp.log1p(jnp.exp(-jnp.abs(x)))


def _mean_last(z, n_axes):
    count = 1
    for ax in range(z.ndim - 1, z.ndim - 1 - n_axes, -1):
        count *= z.shape[ax]
        z = jnp.sum(z, axis=ax, keepdims=True)
    return z * (1.0 / count)


def _layer_norm(z, g, b, n_axes=1):
    mu = _mean_last(z, n_axes)
    zc = z - mu
    var = _mean_last(zc * zc, n_axes)
    return zc * lax.rsqrt(var + LN_EPS) * g + b


def _att_proj_kernel(x_ref, w_ref, bias_ref, gain_ref, cos_ref, sa_ref, sb_ref,
                     qkv_ref, gates_ref, *, n_heads, n_norm_heads):
    x = x_ref[...]
    cos = cos_ref[...]
    sa = sa_ref[...]
    sb = sb_ref[...]
    ones = jnp.ones((x.shape[0], LANES), qkv_ref.dtype)
    for h in range(n_heads):
        if h % 2 == 0:
            pair = jnp.dot(x, w_ref[:, h * LANES:(h + 2) * LANES], preferred_element_type=F32)
        a = pair[:, (h % 2) * LANES:(h % 2 + 1) * LANES]
        if h < n_norm_heads:
            a = a * lax.rsqrt(jnp.mean(a * a, axis=-1, keepdims=True) + QK_EPS) * gain_ref[h:h + 1, :]
            a = a * cos + pltpu.roll(a, 96, 1) * sa + pltpu.roll(a, 32, 1) * sb
            qkv_ref[:, h * LANES:(h + 1) * LANES] = a.astype(qkv_ref.dtype)
        else:
            c = n_norm_heads + 2 * (h - n_norm_heads)
            qkv_ref[:, c * LANES:(c + 1) * LANES] = a.astype(qkv_ref.dtype)
            qkv_ref[:, (c + 1) * LANES:(c + 2) * LANES] = ones
    gates_ref[...] = (jnp.dot(x, w_ref[:, n_heads * LANES:], preferred_element_type=F32)
                      + bias_ref[...])


def _att_proj(xb, w, bias, gain, cos, sa, sb, seqs, tm):
    n, d = xb.shape
    n_heads = (ATT_WIDTH + 2 * KV_WIDTH) // LANES
    ncol = n_heads * LANES + LANES
    assert w.shape == (d, ncol)
    n_out = (n_heads + ATT_KV_HEADS) * LANES

    (b0, t0), (b1, t1) = seqs
    assert t0 % tm == 0 and t1 % tm == 0
    nb0 = b0 * t0 // tm

    def pos_map(i):
        return (jnp.where(i < nb0, i % (t0 // tm), (i - nb0) % (t1 // tm)), 0)

    return pl.pallas_call(
        functools.partial(_att_proj_kernel, n_heads=n_heads, n_norm_heads=ATT_HEADS + ATT_KV_HEADS),
        out_shape=(jax.ShapeDtypeStruct((n, n_out), BF16),
                   jax.ShapeDtypeStruct((n, LANES), F32)),
        grid=(n // tm,),
        in_specs=[pl.BlockSpec((tm, d), lambda i: (i, 0)),
                  pl.BlockSpec((d, ncol), lambda i: (0, 0)),
                  pl.BlockSpec((1, LANES), lambda i: (0, 0)),
                  pl.BlockSpec((16, LANES), lambda i: (0, 0)),
                  pl.BlockSpec((tm, LANES), pos_map),
                  pl.BlockSpec((tm, LANES), pos_map),
                  pl.BlockSpec((tm, LANES), pos_map)],
        out_specs=(pl.BlockSpec((tm, n_out), lambda i: (i, 0)),
                   pl.BlockSpec((tm, LANES), lambda i: (i, 0))),
        compiler_params=_cparams(("parallel",), 48),
        name="att_proj",
    )(xb, w, bias, gain, cos, sa, sb)


def _matmul_kernel(x_ref, w_ref, o_ref):
    o_ref[...] = jnp.dot(x_ref[...], w_ref[...], preferred_element_type=F32).astype(o_ref.dtype)


def _matmul(xb, w, out_dtype, tm, tn):
    n, d = xb.shape
    ncol = w.shape[1]
    assert n % tm == 0 and ncol % tn == 0
    return pl.pallas_call(
        _matmul_kernel,
        out_shape=jax.ShapeDtypeStruct((n, ncol), out_dtype),
        grid=(ncol // tn, n // tm),
        in_specs=[pl.BlockSpec((tm, d), lambda j, i: (i, 0)),
                  pl.BlockSpec((d, tn), lambda j, i: (0, j))],
        out_specs=pl.BlockSpec((tm, tn), lambda j, i: (i, j)),
        compiler_params=_cparams(("parallel", "parallel"), 48),
        name="matmul",
    )(xb, w)


def _attn_kernel(q_ref, k_ref, v_ref, o_ref, m_ref, acc_ref, *, tk, scale):
    bq = q_ref.shape[0]
    t = k_ref.shape[0]
    c = scale * math.log2(math.e)
    m_ref[...] = jnp.full(m_ref.shape, -jnp.inf, F32)
    acc_ref[...] = jnp.zeros(acc_ref.shape, F32)

    def body(ci, carry):
        start = pl.multiple_of(ci * tk, tk)
        k = k_ref[pl.ds(start, tk), :]
        v = v_ref[pl.ds(start, tk), :]
        for h in range(ATT_GROUP):
            r = slice(h * bq, (h + 1) * bq)
            s = lax.dot_general(q_ref[:, h * LANES:(h + 1) * LANES], k, (((1,), (1,)), ((), ())),
                                preferred_element_type=F32)
            m_prev = m_ref[r, :]
            m_new = jnp.maximum(m_prev, jnp.max(s, axis=1, keepdims=True))
            p = jnp.exp2((s - jnp.tile(m_new, (1, tk // LANES))) * c)
            alpha = jnp.exp2((m_prev - m_new) * c)
            pv = jnp.dot(p.astype(BF16), v, preferred_element_type=F32)
            acc_ref[r, :] = jnp.tile(alpha, (1, 2)) * acc_ref[r, :] + pv
            m_ref[r, :] = m_new
        return carry

    lax.fori_loop(0, t // tk, body, 0, unroll=2)
    for h in range(ATT_GROUP):
        r = slice(h * bq, (h + 1) * bq)
        o_ref[:, h * LANES:(h + 1) * LANES] = (acc_ref[r, :LANES] / acc_ref[r, LANES:]).astype(o_ref.dtype)


def _attention(qkv, tok0, nseq, t, bq, tk):
    assert t % bq == 0 and t % tk == 0 and tok0 % t == 0
    qb0 = tok0 // bq
    sb0 = tok0 // t
    nq = t // bq
    gw = ATT_GROUP * LANES
    v_blk0 = (ATT_HEADS + ATT_KV_HEADS) // 2
    return pl.pallas_call(
        functools.partial(_attn_kernel, tk=tk, scale=ATT_HEAD_DIM ** -0.5),
        out_shape=jax.ShapeDtypeStruct((nseq * t, ATT_WIDTH), BF16),
        grid=(nseq, ATT_KV_HEADS, nq),
        in_specs=[pl.BlockSpec((bq, gw), lambda b, g, i: (qb0 + b * nq + i, g)),
                  pl.BlockSpec((t, LANES), lambda b, g, i: (sb0 + b, ATT_HEADS + g)),
                  pl.BlockSpec((t, 2 * LANES), lambda b, g, i: (sb0 + b, v_blk0 + g))],
        out_specs=pl.BlockSpec((bq, gw), lambda b, g, i: (b * nq + i, g)),
        scratch_shapes=[pltpu.VMEM((ATT_GROUP * bq, LANES), F32),
                        pltpu.VMEM((ATT_GROUP * bq, 2 * LANES), F32)],
        compiler_params=_cparams(("parallel", "parallel", "parallel"), 40),
        name="attention",
    )(qkv, qkv, qkv)


def _mlstm_chains(chains):
    C = range(len(chains))
    ds = [c[0] for c in chains]
    q, k, kt, v, gt = ([c[i] for c in chains] for i in (2, 3, 4, 5, 6))
    h_ref, c_ref, n_ref, m_ref = ([c[i] for c in chains] for i in (7, 8, 9, 10))
    L, dk = q[0].shape
    ci = [c[0] * ML_HEADS + c[1] for c in chains]
    ig_row = [gt[i][ci[i]:ci[i] + 1, :] for i in C]
    logf_row = [_log_sigmoid(gt[i][2 * ML_HEADS + ci[i]:2 * ML_HEADS + ci[i] + 1, :]) for i in C]

    row = lax.broadcasted_iota(I32, (L, L), 0)
    col = lax.broadcasted_iota(I32, (L, L), 1)
    masks = (col <= row, col >= row)
    mask = [masks[ds[i]] for i in C]
    b_col = [jnp.sum(jnp.where(mask[i], logf_row[i], 0.0), axis=1, keepdims=True) for i in C]
    b_row = [jnp.sum(jnp.where(row == col, b_col[i], 0.0), axis=0, keepdims=True) for i in C]
    gtot = [jnp.sum(logf_row[i], axis=1, keepdims=True) for i in C]

    m_prev = [m_ref[i][...] for i in C]
    scale = dk ** -0.5
    assert math.log2(dk) % 2 == 0

    a_row = [gtot[i] - b_row[i] + ig_row[i] for i in C]
    m_loc = [jnp.max(a_row[i], axis=1, keepdims=True) for i in C]
    w_row = [jnp.exp(a_row[i] - m_loc[i]) * scale for i in C]
    c_loc = [jnp.dot((kt[i].astype(F32) * w_row[i]).astype(BF16), v[i], preferred_element_type=F32)
             for i in C]
    n_loc = [jnp.dot(jnp.broadcast_to(w_row[i], (8, L)).astype(BF16), k[i],
                     preferred_element_type=F32)[0:1, :] for i in C]

    dmat = [jnp.where(mask[i], b_col[i] - b_row[i] + ig_row[i], -jnp.inf) for i in C]
    inter = [b_col[i] + m_prev[i] for i in C]
    m_j = [jnp.maximum(jnp.max(dmat[i], axis=1, keepdims=True), inter[i]) for i in C]
    qk = [jnp.dot(q[i], kt[i], preferred_element_type=F32) for i in C]
    s = [qk[i] * (jnp.exp(dmat[i] - m_j[i]) * scale) for i in C]
    s_int = [jnp.exp(inter[i] - m_j[i]) for i in C]
    qc = [jnp.dot(q[i], c_ref[i][...].astype(BF16), preferred_element_type=F32) for i in C]
    num = [jnp.dot(s[i].astype(BF16), v[i], preferred_element_type=F32) + s_int[i] * qc[i] for i in C]
    den = [jnp.sum(s[i], axis=1, keepdims=True)
           + s_int[i] * jnp.sum(q[i].astype(F32) * n_ref[i][...], axis=1, keepdims=True) for i in C]
    for i in C:
        h_ref[i][...] = num[i] / jnp.maximum(jnp.abs(den[i]), jnp.exp(-m_j[i]))

    m_new = [jnp.maximum(gtot[i] + m_prev[i], m_loc[i]) for i in C]
    s_old = [jnp.exp(gtot[i] + m_prev[i] - m_new[i]) for i in C]
    s_new = [jnp.exp(m_loc[i] - m_new[i]) for i in C]
    for i in C:
        c_ref[i][...] = s_old[i] * c_ref[i][...] + s_new[i] * c_loc[i]
        n_ref[i][...] = s_old[i] * n_ref[i][...] + s_new[i] * n_loc[i]
        m_ref[i][...] = m_new[i]


def _mlstm_kernel(first_ref, qf_ref, kf_ref, vf_ref, ktf_ref, gtf_ref, qb_ref, kb_ref, vb_ref, ktb_ref,
                  gtb_ref, hf_ref, hb_ref, c_ref, n_ref, m_ref):
    i = pl.program_id(0)
    L = ML_CHUNK
    n_sub = qf_ref.shape[0] // L
    dirs = ((qf_ref, kf_ref, vf_ref, ktf_ref, gtf_ref, hf_ref),
            (qb_ref, kb_ref, vb_ref, ktb_ref, gtb_ref, hb_ref))
    for d in range(2):
        @pl.when(first_ref[d, i] == 1)
        def _():
            for hd in range(ML_HEADS):
                s = d * ML_HEADS + hd
                c_ref[s] = jnp.zeros(c_ref.shape[1:], F32)
                n_ref[s] = jnp.zeros(n_ref.shape[1:], F32)
                m_ref[s] = jnp.zeros(m_ref.shape[1:], F32)

    for sub in range(n_sub):
        for d, (q_ref, k_ref, v_ref, kt_ref, gt_ref, h_ref) in enumerate(dirs):
            rows = pl.ds((sub if d == 0 else n_sub - 1 - sub) * L, L)
            gt = gt_ref[:, rows]
            chains = []
            for hd in range(ML_HEADS):
                s = d * ML_HEADS + hd
                cols = slice(hd * ML_HEAD_DIM, (hd + 1) * ML_HEAD_DIM)
                chains.append((d, hd, q_ref[rows, cols], k_ref[rows, cols], kt_ref[cols, rows],
                               v_ref[rows, cols], gt, h_ref.at[rows, cols], c_ref.at[s], n_ref.at[s],
                               m_ref.at[s]))
            for g in range(0, ML_HEADS, ML_INTERLEAVE):
                _mlstm_chains(chains[g:g + ML_INTERLEAVE])


def _mlstm(rest, k_t, gates_t, seqs, n_sub):
    n = rest.shape[0]
    L = n_sub * ML_CHUNK
    nc = n // L
    first_f = np.zeros((nc,), np.int32)
    last_f = np.zeros((nc,), np.int32)
    tok = 0
    for (b, t) in seqs:
        assert t % L == 0
        for _ in range(b):
            first_f[tok // L] = 1
            last_f[(tok + t) // L - 1] = 1
            tok += t
    first = np.stack([first_f, last_f[::-1]]).astype(np.int32)
    W = ML_WIDTH
    nstate = 2 * ML_HEADS

    def fwd(c):
        return lambda i, fst: (i, c)

    def bwd(c):
        return lambda i, fst: (nc - 1 - i, c)

    def specs(m, mt):
        return [pl.BlockSpec((L, W), m(0)), pl.BlockSpec((L, W), m(1)), pl.BlockSpec((L, W), m(2)),
                pl.BlockSpec((W, L), mt), pl.BlockSpec((N_ML_GATES, L), mt)]

    grid_spec = pltpu.PrefetchScalarGridSpec(
        num_scalar_prefetch=1,
        grid=(nc,),
        in_specs=(specs(fwd, lambda i, fst: (0, i)) + specs(bwd, lambda i, fst: (0, nc - 1 - i))),
        out_specs=(pl.BlockSpec((L, W), fwd(0)), pl.BlockSpec((L, W), bwd(0))),
        scratch_shapes=[pltpu.VMEM((nstate, ML_HEAD_DIM, ML_HEAD_DIM), F32),
                        pltpu.VMEM((nstate, 1, ML_HEAD_DIM), F32),
                        pltpu.VMEM((nstate, 1, 1), F32)],
    )
    return pl.pallas_call(
        _mlstm_kernel,
        out_shape=(jax.ShapeDtypeStruct((n, W), F32), jax.ShapeDtypeStruct((n, W), F32)),
        grid_spec=grid_spec,
        compiler_params=_cparams(("arbitrary",), 32),
        name="mlstm",
    )(jnp.asarray(first), rest, rest, rest, k_t, gates_t, rest, rest, rest, k_t, gates_t)


def _merge_kernel(a_ref, hf_ref, hb_ref, mo_ref, ga_ref, gm_ref, wa_ref, wm_ref, wo_ref,
                  lng_ref, lnb_ref, *refs, alpha, split_block):
    x_refs, (xo_ref, xob_ref) = refs[:-2], refs[-2:]
    if split_block is None:
        x = x_refs[0][...]
    else:
        x = jnp.where(pl.program_id(0) < split_block, x_refs[0][...], x_refs[1][...])
    m_out = (_sigmoid(mo_ref[...].astype(F32)) * (hf_ref[...] + hb_ref[...])).astype(BF16)
    pa = jnp.dot(a_ref[...], wa_ref[...], preferred_element_type=F32)
    pm = jnp.dot(m_out, wm_ref[...], preferred_element_type=F32)
    merged = _sigmoid(ga_ref[...].astype(F32)) * pa + _sigmoid(gm_ref[...].astype(F32)) * pm
    y = jnp.dot(merged.astype(BF16), wo_ref[...], preferred_element_type=F32)
    z = _layer_norm(alpha * x + y, lng_ref[...], lnb_ref[...])
    xo_ref[...] = z
    xob_ref[...] = z.astype(BF16)


def _merge(a_out, hf, hb, rest, xs, wa, wm, wo, lng, lnb, alpha, tm):
    n, d = a_out.shape[0], wo.shape[1]
    assert n % tm == 0 and d == 2 * ML_WIDTH and ATT_WIDTH == ML_WIDTH
    const = dict(pipeline_mode=pl.Buffered(1))
    if len(xs) == 1:
        sb = None
        x_specs = [pl.BlockSpec((tm, d), lambda i: (i, 0))]
    else:
        assert len(xs) == 2 and xs[0].shape[0] % tm == 0 and xs[0].shape[0] + xs[1].shape[0] == n
        sb = xs[0].shape[0] // tm
        x_specs = [pl.BlockSpec((tm, d), lambda i: (jnp.minimum(i, sb - 1), 0)),
                   pl.BlockSpec((tm, d), lambda i: (jnp.maximum(i - sb, 0), 0))]
    return pl.pallas_call(
        functools.partial(_merge_kernel, alpha=alpha, split_block=sb),
        out_shape=(jax.ShapeDtypeStruct((n, d), F32), jax.ShapeDtypeStruct((n, d), BF16)),
        grid=(n // tm,),
        in_specs=[pl.BlockSpec((tm, ATT_WIDTH), lambda i: (i, 0)),
                  pl.BlockSpec((tm, ML_WIDTH), lambda i: (i, 0)),
                  pl.BlockSpec((tm, ML_WIDTH), lambda i: (i, 0)),
                  pl.BlockSpec((tm, ML_WIDTH), lambda i: (i, 3)),
                  pl.BlockSpec((tm, d), lambda i: (i, 2)),
                  pl.BlockSpec((tm, d), lambda i: (i, 3)),
                  pl.BlockSpec((ATT_WIDTH, d), lambda i: (0, 0), **const),
                  pl.BlockSpec((ML_WIDTH, d), lambda i: (0, 0), **const),
                  pl.BlockSpec((d, d), lambda i: (0, 0), **const),
                  pl.BlockSpec((1, d), lambda i: (0, 0)),
                  pl.BlockSpec((1, d), lambda i: (0, 0)),
                  *x_specs],
        out_specs=(pl.BlockSpec((tm, d), lambda i: (i, 0)),
                   pl.BlockSpec((tm, d), lambda i: (i, 0))),
        compiler_params=_cparams(("parallel",), 56),
        name="merge",
    )(a_out, hf, hb, rest, rest, rest, wa, wm, wo, lng, lnb, *xs)


def _ffn_kernel(xb_ref, w1_ref, w3_ref, w2_ref, x_ref, lng_ref, lnb_ref, *refs, alpha, n_cast):
    cast_in = refs[:n_cast]
    xo_ref, xob_ref = refs[n_cast:n_cast + 2]
    cast_out = refs[n_cast + 2:2 * n_cast + 2]
    acc_ref = refs[-1]
    j = pl.program_id(1)

    for src, dst in zip(cast_in, cast_out):
        dst[...] = src[...].astype(BF16)

    @pl.when(j == 0)
    def _():
        acc_ref[...] = jnp.zeros(acc_ref.shape, F32)

    xb = xb_ref[...]
    a = jnp.dot(xb, w1_ref[...], preferred_element_type=F32)
    b = jnp.dot(xb, w3_ref[...], preferred_element_type=F32)
    hid = (a * _sigmoid(a) * b).astype(BF16)
    acc_ref[...] += jnp.dot(hid, w2_ref[...], preferred_element_type=F32)

    @pl.when(j == pl.num_programs(1) - 1)
    def _():
        z = _layer_norm(alpha * x_ref[...] + acc_ref[...], lng_ref[...], lnb_ref[...])
        xo_ref[...] = z
        xob_ref[...] = z.astype(BF16)


def _cast_block_rows(rows, steps):
    rb = 16 * pl.cdiv(pl.cdiv(rows, steps), 16)
    while rows % rb:
        rb += 16
    return rb


def _ffn(xb, x, w1, w3, w2, lng, lnb, alpha, tm, tf, to_cast=()):
    n, d = x.shape
    f = w1.shape[1]
    assert n % tm == 0 and f % tf == 0
    nj = f // tf
    steps = (n // tm) * nj
    cast_specs = []
    for a in to_cast:
        rb = _cast_block_rows(a.shape[0], steps)
        last = a.shape[0] // rb - 1
        cast_specs.append(pl.BlockSpec((rb, a.shape[1]),
                                       lambda i, j, last=last: (jnp.minimum(i * nj + j, last), 0)))
    outs = pl.pallas_call(
        functools.partial(_ffn_kernel, alpha=alpha, n_cast=len(to_cast)),
        out_shape=(jax.ShapeDtypeStruct((n, d), F32), jax.ShapeDtypeStruct((n, d), BF16),
                   *[jax.ShapeDtypeStruct(a.shape, BF16) for a in to_cast]),
        grid=(n // tm, nj),
        in_specs=[pl.BlockSpec((tm, d), lambda i, j: (i, 0)),
                  pl.BlockSpec((d, tf), lambda i, j: (0, j)),
                  pl.BlockSpec((d, tf), lambda i, j: (0, j)),
                  pl.BlockSpec((tf, d), lambda i, j: (j, 0)),
                  pl.BlockSpec((tm, d), lambda i, j: (i, 0)),
                  pl.BlockSpec((1, d), lambda i, j: (0, 0)),
                  pl.BlockSpec((1, d), lambda i, j: (0, 0)),
                  *cast_specs],
        out_specs=(pl.BlockSpec((tm, d), lambda i, j: (i, 0)),
                   pl.BlockSpec((tm, d), lambda i, j: (i, 0)),
                   *cast_specs),
        scratch_shapes=[pltpu.VMEM((tm, d), F32)],
        compiler_params=_cparams(("arbitrary", "arbitrary"), 56),
        name="ffn_dense",
    )(xb, w1, w3, w2, x, lng, lnb, *to_cast)
    return outs[0], outs[1], outs[2:]


def _router_kernel(x_ref, wr_ref, tri_ref, ints_ref, flts_ref, cnt_ref, base_ref):
    i = pl.program_id(0)

    @pl.when(i == 0)
    def _():
        base_ref[...] = jnp.zeros(base_ref.shape, F32)

    logits = jnp.dot(x_ref[...], wr_ref[...], preferred_element_type=F32,
                     precision=lax.Precision.HIGHEST)
    lane = lax.broadcasted_iota(I32, logits.shape, 1)
    lanef = lane.astype(F32)
    logits = jnp.where(lane < N_EXPERTS, logits, -jnp.inf)
    v0 = jnp.max(logits, axis=1, keepdims=True)
    i0 = jnp.min(jnp.where(logits == v0, lanef, float(LANES)), axis=1, keepdims=True)
    rest = jnp.where(lanef == i0, -jnp.inf, logits)
    v1 = jnp.max(rest, axis=1, keepdims=True)
    i1 = jnp.min(jnp.where(rest == v1, lanef, float(LANES)), axis=1, keepdims=True)
    e1 = jnp.exp(v1 - v0)
    g0 = 1.0 / (1.0 + e1)
    g1 = e1 / (1.0 + e1)

    sel0 = lanef == i0
    sel1 = lanef == i1
    onehot = jnp.where(sel0 | sel1, 1.0, 0.0)
    before = jnp.dot(tri_ref[...], onehot.astype(BF16), preferred_element_type=F32) + base_ref[...]
    r0 = jnp.sum(jnp.where(sel0, before, 0.0), axis=1, keepdims=True)
    r1 = jnp.sum(jnp.where(sel1, before, 0.0), axis=1, keepdims=True)
    base_ref[...] += jnp.sum(onehot, axis=0, keepdims=True)
    cnt_ref[...] = jnp.broadcast_to(base_ref[...], cnt_ref.shape)

    ints = jnp.where(lane == 0, i0, jnp.where(lane == 1, i1, jnp.where(lane == 2, r0,
                     jnp.where(lane == 3, r1, 0.0))))
    ints_ref[...] = ints.astype(I32)
    flts_ref[...] = jnp.where(lane == 0, g0, jnp.where(lane == 1, g1, 0.0))


def _router(x, wr, tr):
    n, d = x.shape
    assert n % tr == 0
    tri = jnp.asarray(np.tril(np.ones((tr, tr), np.float32), -1), BF16)
    return pl.pallas_call(
        _router_kernel,
        out_shape=(jax.ShapeDtypeStruct((n, LANES), I32),
                   jax.ShapeDtypeStruct((n, LANES), F32),
                   jax.ShapeDtypeStruct((8, LANES), F32)),
        grid=(n // tr,),
        in_specs=[pl.BlockSpec((tr, d), lambda i: (i, 0)),
                  pl.BlockSpec((d, LANES), lambda i: (0, 0)),
                  pl.BlockSpec((tr, tr), lambda i: (0, 0))],
        out_specs=(pl.BlockSpec((tr, LANES), lambda i: (i, 0)),
                   pl.BlockSpec((tr, LANES), lambda i: (i, 0)),
                   pl.BlockSpec((8, LANES), lambda i: (0, 0))),
        scratch_shapes=[pltpu.VMEM((1, LANES), F32)],
        compiler_params=_cparams(("arbitrary",), 32),
        name="moe_router",
    )(x, wr, tri)


def _row_dma_loop(make_copy, tt):
    def issue(t, carry):
        for k in range(TOP_K):
            make_copy(t, k).start()
        return carry

    def drain(t, carry):
        for k in range(TOP_K):
            make_copy(t, k).wait()
        return carry

    lax.fori_loop(0, tt, issue, 0)
    lax.fori_loop(0, tt, drain, 0)


def _dispatch_kernel(pos_ref, x_ref, xs_in_ref, xs_ref, sem):
    del xs_in_ref

    def copy(t, k):
        return pltpu.make_async_copy(x_ref.at[pl.ds(t, 1), :],
                                     xs_ref.at[pl.ds(pos_ref[0, k, t], 1), :], sem)

    _row_dma_loop(copy, x_ref.shape[0])


def _dispatch(x, pos, n_rows, tt):
    n, d = x.shape
    xs0 = jnp.zeros((n_rows, d), x.dtype)
    return pl.pallas_call(
        _dispatch_kernel,
        out_shape=jax.ShapeDtypeStruct(xs0.shape, xs0.dtype),
        grid=(n // tt,),
        in_specs=[pl.BlockSpec((1, TOP_K, tt), lambda i: (i, 0, 0), memory_space=pltpu.SMEM),
                  pl.BlockSpec((tt, d), lambda i: (i, 0)),
                  pl.BlockSpec(memory_space=pl.ANY)],
        out_specs=pl.BlockSpec(memory_space=pl.ANY),
        scratch_shapes=[pltpu.SemaphoreType.DMA],
        input_output_aliases={2: 0},
        compiler_params=_cparams(("arbitrary",), 32),
        name="moe_dispatch",
    )(pos, x, xs0)


def _moe_ffn_kernel(te_ref, nv_ref, xs_ref, w1_ref, w3_ref, w2_ref, y_ref, xb_ref):
    i = pl.program_id(0)
    j = pl.program_id(1)

    @pl.when(i < nv_ref[0])
    def _():
        @pl.when(j == 0)
        def _():
            xb_ref[...] = xs_ref[...].astype(BF16)
            y_ref[...] = jnp.zeros(y_ref.shape, F32)

        xb = xb_ref[...]
        a = jnp.dot(xb, w1_ref[...], preferred_element_type=F32)
        b = jnp.dot(xb, w3_ref[...], preferred_element_type=F32)
        hid = (a * _sigmoid(a) * b).astype(BF16)
        y_ref[...] += jnp.dot(hid, w2_ref[...], preferred_element_type=F32)

    @pl.when((i >= nv_ref[0]) & (j == 0))
    def _():
        y_ref[...] = jnp.zeros(y_ref.shape, F32)


def _moe_ffn(xs, tile_expert, n_valid, w1, w3, w2, tm, tf):
    p, d = xs.shape
    f = w1.shape[2]
    assert p % tm == 0 and f % tf == 0
    nj = f // tf

    def row_map(i, j, te, nv):
        return (jnp.minimum(i, nv[0] - 1), 0)

    def jj(i, j, nv):
        return jnp.where(i < nv[0], j, nj - 1)

    grid_spec = pltpu.PrefetchScalarGridSpec(
        num_scalar_prefetch=2,
        grid=(p // tm, nj),
        in_specs=[pl.BlockSpec((tm, d), row_map),
                  pl.BlockSpec((None, d, tf), lambda i, j, te, nv: (te[i], 0, jj(i, j, nv))),
                  pl.BlockSpec((None, d, tf), lambda i, j, te, nv: (te[i], 0, jj(i, j, nv))),
                  pl.BlockSpec((None, tf, d), lambda i, j, te, nv: (te[i], jj(i, j, nv), 0))],
        out_specs=pl.BlockSpec((tm, d), lambda i, j, te, nv: (i, 0)),
        scratch_shapes=[pltpu.VMEM((tm, d), BF16)],
    )
    return pl.pallas_call(
        _moe_ffn_kernel,
        out_shape=jax.ShapeDtypeStruct((p, d), F32),
        grid_spec=grid_spec,
        compiler_params=_cparams(("arbitrary", "arbitrary"), 56),
        name="moe_ffn",
    )(tile_expert, n_valid, xs, w1, w3, w2)


def _combine_kernel(pos_ref, y_ref, g_ref, x_ref, lng_ref, lnb_ref, *refs, alpha, split_block):
    out_refs = refs[:-3]
    buf0_ref, buf1_ref, sem = refs[-3:]

    def copy(t, k):
        buf = buf0_ref if k == 0 else buf1_ref
        return pltpu.make_async_copy(y_ref.at[pl.ds(pos_ref[0, k, t], 1), :], buf.at[pl.ds(t, 1), :], sem)

    _row_dma_loop(copy, x_ref.shape[0])
    f = g_ref[:, 0:1] * buf0_ref[...] + g_ref[:, 1:2] * buf1_ref[...]
    z = _layer_norm(alpha * x_ref[...] + f, lng_ref[...], lnb_ref[...])
    if split_block is None:
        out_refs[0][...] = z
    else:
        @pl.when(pl.program_id(0) < split_block)
        def _():
            out_refs[0][...] = z

        @pl.when(pl.program_id(0) >= split_block)
        def _():
            out_refs[1][...] = z


def _combine(y, pos, gates, x, lng, lnb, alpha, tt, split=None):
    n, d = x.shape
    if split is None:
        sb = None
        out_shape = jax.ShapeDtypeStruct((n, d), F32)
        out_specs = pl.BlockSpec((tt, d), lambda i: (i, 0))
    else:
        assert split % tt == 0 and 0 < split < n
        sb = split // tt
        out_shape = (jax.ShapeDtypeStruct((split, d), F32), jax.ShapeDtypeStruct((n - split, d), F32))
        out_specs = (pl.BlockSpec((tt, d), lambda i: (jnp.minimum(i, sb - 1), 0)),
                     pl.BlockSpec((tt, d), lambda i: (jnp.maximum(i - sb, 0), 0)))
    return pl.pallas_call(
        functools.partial(_combine_kernel, alpha=alpha, split_block=sb),
        out_shape=out_shape,
        grid=(n // tt,),
        in_specs=[pl.BlockSpec((1, TOP_K, tt), lambda i: (i, 0, 0), memory_space=pltpu.SMEM),
                  pl.BlockSpec(memory_space=pl.ANY),
                  pl.BlockSpec((tt, LANES), lambda i: (i, 0)),
                  pl.BlockSpec((tt, d), lambda i: (i, 0)),
                  pl.BlockSpec((1, d), lambda i: (0, 0)),
                  pl.BlockSpec((1, d), lambda i: (0, 0))],
        out_specs=out_specs,
        scratch_shapes=[pltpu.VMEM((tt, d), F32), pltpu.VMEM((tt, d), F32), pltpu.SemaphoreType.DMA],
        compiler_params=_cparams(("arbitrary",), 32),
        name="moe_combine",
    )(pos, y, gates, x, lng, lnb)


def _moe(x, wr, w1, w3, w2, lng, lnb, alpha, tm, tf, tr, tt, split=None):
    n, d = x.shape
    ints, flts, cnt = _router(x, wr, tr)
    e0, e1, r0, r1 = ints[:, 0], ints[:, 1], ints[:, 2], ints[:, 3]
    counts = cnt[0, :N_EXPERTS].astype(I32)

    tiles = (counts + tm - 1) // tm
    tile_end = jnp.cumsum(tiles)
    offs = (tile_end - tiles) * tm
    n_tiles = (TOP_K * n) // tm + N_EXPERTS
    n_valid = tile_end[-1:]
    tile_expert = jnp.sum(jnp.arange(n_tiles, dtype=I32)[:, None] >= tile_end[None, :], axis=1)
    tile_expert = jnp.minimum(tile_expert, N_EXPERTS - 1).astype(I32)
    tile_expert = jnp.where(jnp.arange(n_tiles) < n_valid[0], tile_expert,
                            tile_expert[jnp.maximum(n_valid[0] - 1, 0)])
    pos = jnp.stack([offs[e0] + r0, offs[e1] + r1], axis=0)
    pos = pos.reshape(TOP_K, n // tt, tt).transpose(1, 0, 2)

    xs = _dispatch(x, pos, n_tiles * tm, tt)
    y = _moe_ffn(xs, tile_expert, n_valid.astype(I32), w1, w3, w2, tm, tf)
    return _combine(y, pos, flts, x, lng[None, :], lnb[None, :], alpha, tt, split=split)


def _rope_tables(t_max):
    half = ATT_HEAD_DIM // 2
    pos = np.arange(t_max)
    pos_r = (pos // GRID_W).astype(np.float32)
    pos_c = (pos % GRID_W).astype(np.float32)
    inv = jnp.asarray(ROPE_THETA, F32) ** (-jnp.arange(0, half, 2, dtype=F32) / half)
    ang_r = jnp.asarray(pos_r)[:, None] * inv
    ang_c = jnp.asarray(pos_c)[:, None] * inv
    cr, sr, cc, sc = jnp.cos(ang_r), jnp.sin(ang_r), jnp.cos(ang_c), jnp.sin(ang_c)
    zero = jnp.zeros_like(sr)
    cos = jnp.concatenate([cr, cr, cc, cc], axis=1)
    sin_up = jnp.concatenate([-sr, zero, -sc, zero], axis=1)
    sin_dn = jnp.concatenate([zero, sr, zero, sc], axis=1)
    return cos, sin_up, sin_dn


def _split_w_in(w_in_l):
    sizes = (ATT_WIDTH, KV_WIDTH, KV_WIDTH, ML_WIDTH, ML_WIDTH, ML_WIDTH, ML_WIDTH, N_ML_GATES)
    o = np.cumsum((0,) + sizes)
    d = w_in_l.shape[0]
    wb = lax.optimization_barrier(w_in_l.astype(BF16))
    w_gate = jnp.concatenate([wb[:, o[7]:o[8]], jnp.zeros((d, LANES - N_ML_GATES), BF16)], axis=1)
    w_att = jnp.concatenate([wb[:, o[0]:o[3]], w_gate], axis=1)
    w_rest = jnp.concatenate([wb[:, o[3]:o[7]], wb[:, o[8]:]], axis=1)
    return w_att, w_rest


def kernel(x_prompt, x_sample, w_in, b_gates, q_gain, k_gain, w_att_br, w_ml_br, w_o, ln1_g, ln1_b,
           w1_d, w3_d, w2_d, router, e_w1, e_w3, e_w2, ln2_g, ln2_b):
    depth = w_in.shape[0]
    d = x_prompt.shape[-1]
    seqs = ((x_prompt.shape[0], x_prompt.shape[1]), (x_sample.shape[0], x_sample.shape[1]))
    n0 = seqs[0][0] * seqs[0][1]
    alpha = float((2 * depth) ** 0.25)

    x = [x_prompt.reshape(-1, d), x_sample.reshape(-1, d)]
    xb = jnp.concatenate([x[0].astype(BF16), x[1].astype(BF16)], axis=0)
    cos, sin_up, sin_dn = _rope_tables(max(seqs[0][1], seqs[1][1]))

    for l in range(depth):
        w_att, w_rest = _split_w_in(w_in[l])
        bias = jnp.concatenate([b_gates[l], jnp.zeros((LANES - N_ML_GATES,), F32)])[None, :]
        gain = jnp.concatenate([jnp.broadcast_to(q_gain[l], (ATT_HEADS, ATT_HEAD_DIM)),
                                jnp.broadcast_to(k_gain[l], (ATT_KV_HEADS, ATT_HEAD_DIM)),
                                jnp.zeros((16 - ATT_HEADS - ATT_KV_HEADS, ATT_HEAD_DIM), F32)], axis=0)

        T = TILES
        qkv, gates = _att_proj(xb, w_att, bias, gain, cos, sin_up, sin_dn, seqs, tm=T["att_proj_tm"])
        rest = _matmul(xb, w_rest, BF16, tm=T["proj_tm"], tn=T["proj_tn"])

        a_out = jnp.concatenate(
            [_attention(qkv, 0, seqs[0][0], seqs[0][1], bq=min(T["attn_bq"], seqs[0][1]), tk=T["attn_tk"]),
             _attention(qkv, n0, seqs[1][0], seqs[1][1], bq=min(T["attn_bq"], seqs[1][1]), tk=T["attn_tk"])],
            axis=0)
        hf, hb = _mlstm(rest, rest[:, ML_WIDTH:2 * ML_WIDTH].T, gates[:, :N_ML_GATES].T, seqs,
                        n_sub=T["mlstm_chunks"])

        x, xb = _merge(a_out, hf, hb, rest, x if l == 0 else [x], w_att_br[l].astype(BF16), w_ml_br[l].astype(BF16),
                       w_o[l].astype(BF16), ln1_g[l][None, :], ln1_b[l][None, :], alpha, tm=T["merge_tm"])

        j = l // 2
        if l % 2 == 0:
            nxt = (e_w1[j], e_w3[j], e_w2[j]) if l + 1 < depth else ()
            x, xb, cast = _ffn(xb, x, w1_d[j].astype(BF16), w3_d[j].astype(BF16), w2_d[j].astype(BF16),
                               ln2_g[l][None, :], ln2_b[l][None, :], alpha, tm=T["ffn_tm"], tf=T["ffn_tf"],
                               to_cast=[e.reshape(-1, e.shape[-1]) for e in nxt])
            expert_w = [c.reshape(e.shape) for c, e in zip(cast, nxt)]
        else:
            wr = jnp.concatenate([router[j], jnp.zeros((d, LANES - N_EXPERTS), F32)], axis=1)
            last = l == depth - 1
            x = _moe(x, wr, *expert_w, ln2_g[l], ln2_b[l], alpha, tm=T["moe_tm"], tf=T["moe_tf"],
                     tr=T["router_tr"], tt=T["moe_tt"], split=n0 if last else None)
            if last:
                return (x[0].reshape(x_prompt.shape), x[1].reshape(x_sample.shape))
            xb = x.astype(BF16)

    return (x[:n0].reshape(x_prompt.shape), x[n0:].reshape(x_sample.shape))
```

```python
import functools
import math

import numpy as np
import jax
import jax.numpy as jnp
from jax import lax
from jax.experimental import pallas as pl
from jax.experimental.pallas import tpu as pltpu

F32 = jnp.float32
BF16 = jnp.bfloat16
I32 = jnp.int32

GRID_W = 64
ATT_HEADS = 8
ATT_KV_HEADS = 2
ATT_HEAD_DIM = 128
ATT_GROUP = ATT_HEADS // ATT_KV_HEADS
ROPE_THETA = 10000.0
QK_EPS = 1e-6
ML_HEADS = 4
ML_HEAD_DIM = 256
ML_CHUNK = 128
ML_INTERLEAVE = 2
COMBINE_PARTS = 2
ATT_WIDTH = ATT_HEADS * ATT_HEAD_DIM
KV_WIDTH = ATT_KV_HEADS * ATT_HEAD_DIM
ML_WIDTH = ML_HEADS * ML_HEAD_DIM
N_ML_GATES = 4 * ML_HEADS
N_EXPERTS = 8
TOP_K = 2
LN_EPS = 1e-5

LANES = 128
V7X_VMEM_BYTES = 64 * 2 ** 20
MIB = 2 ** 20

TILES = dict(att_proj_tm=512, proj_tm=1024, proj_tn=1024, attn_bq=1024, attn_tk=512, merge_tm=256,
             ffn_tm=512, ffn_tf=512, moe_tm=512, moe_tf=1024, router_tr=512, moe_tt=512, mlstm_chunks=2)


def _cparams(semantics, vmem_mib):
    assert vmem_mib * MIB < V7X_VMEM_BYTES
    return pltpu.CompilerParams(dimension_semantics=semantics, vmem_limit_bytes=vmem_mib * MIB)


def _sigmoid(x):
    return 1.0 / (1.0 + jnp.exp(-x))


def _log_sigmoid(x):
    return jnp.minimum(x, 0.0) - jnp.log1p(jnp.exp(-jnp.abs(x)))


def _mean_last(z, n_axes):
    count = 1
    for ax in range(z.ndim - 1, z.ndim - 1 - n_axes, -1):
        count *= z.shape[ax]
        z = jnp.sum(z, axis=ax, keepdims=True)
    return z * (1.0 / count)


def _layer_norm(z, g, b, n_axes=1):
    mu = _mean_last(z, n_axes)
    zc = z - mu
    var = _mean_last(zc * zc, n_axes)
    return zc * lax.rsqrt(var + LN_EPS) * g + b


def _att_proj_kernel(x_ref, w_ref, bias_ref, gain_ref, cos_ref, sa_ref, sb_ref,
                     qkv_ref, gates_ref, *, n_heads, n_norm_heads):
    x = x_ref[...]
    cos = cos_ref[...]
    sa = sa_ref[...]
    sb = sb_ref[...]
    ones = jnp.ones((x.shape[0], LANES), qkv_ref.dtype)
    for h in range(n_heads):
        if h % 2 == 0:
            pair = jnp.dot(x, w_ref[:, h * LANES:(h + 2) * LANES], preferred_element_type=F32)
        a = pair[:, (h % 2) * LANES:(h % 2 + 1) * LANES]
        if h < n_norm_heads:
            a = a * lax.rsqrt(jnp.mean(a * a, axis=-1, keepdims=True) + QK_EPS) * gain_ref[h:h + 1, :]
            a = a * cos + pltpu.roll(a, 96, 1) * sa + pltpu.roll(a, 32, 1) * sb
            qkv_ref[:, h * LANES:(h + 1) * LANES] = a.astype(qkv_ref.dtype)
        else:
            c = n_norm_heads + 2 * (h - n_norm_heads)
            qkv_ref[:, c * LANES:(c + 1) * LANES] = a.astype(qkv_ref.dtype)
            qkv_ref[:, (c + 1) * LANES:(c + 2) * LANES] = ones
    gates_ref[...] = (jnp.dot(x, w_ref[:, n_heads * LANES:], preferred_element_type=F32)
                      + bias_ref[...])


def _att_proj(xb, w, bias, gain, cos, sa, sb, seqs, tm):
    n, d = xb.shape
    n_heads = (ATT_WIDTH + 2 * KV_WIDTH) // LANES
    ncol = n_heads * LANES + LANES
    assert w.shape == (d, ncol)
    n_out = (n_heads + ATT_KV_HEADS) * LANES

    (b0, t0), (b1, t1) = seqs
    assert t0 % tm == 0 and t1 % tm == 0
    nb0 = b0 * t0 // tm

    def pos_map(i):
        return (jnp.where(i < nb0, i % (t0 // tm), (i - nb0) % (t1 // tm)), 0)

    return pl.pallas_call(
        functools.partial(_att_proj_kernel, n_heads=n_heads, n_norm_heads=ATT_HEADS + ATT_KV_HEADS),
        out_shape=(jax.ShapeDtypeStruct((n, n_out), BF16),
                   jax.ShapeDtypeStruct((n, LANES), F32)),
        grid=(n // tm,),
        in_specs=[pl.BlockSpec((tm, d), lambda i: (i, 0)),
                  pl.BlockSpec((d, ncol), lambda i: (0, 0)),
                  pl.BlockSpec((1, LANES), lambda i: (0, 0)),
                  pl.BlockSpec((16, LANES), lambda i: (0, 0)),
                  pl.BlockSpec((tm, LANES), pos_map),
                  pl.BlockSpec((tm, LANES), pos_map),
                  pl.BlockSpec((tm, LANES), pos_map)],
        out_specs=(pl.BlockSpec((tm, n_out), lambda i: (i, 0)),
                   pl.BlockSpec((tm, LANES), lambda i: (i, 0))),
        compiler_params=_cparams(("parallel",), 48),
        name="att_proj",
    )(xb, w, bias, gain, cos, sa, sb)


def _matmul_kernel(x_ref, w_ref, o_ref):
    o_ref[...] = jnp.dot(x_ref[...], w_ref[...], preferred_element_type=F32).astype(o_ref.dtype)


def _matmul(xb, w, out_dtype, tm, tn):
    n, d = xb.shape
    ncol = w.shape[1]
    assert n % tm == 0 and ncol % tn == 0
    return pl.pallas_call(
        _matmul_kernel,
        out_shape=jax.ShapeDtypeStruct((n, ncol), out_dtype),
        grid=(ncol // tn, n // tm),
        in_specs=[pl.BlockSpec((tm, d), lambda j, i: (i, 0)),
                  pl.BlockSpec((d, tn), lambda j, i: (0, j))],
        out_specs=pl.BlockSpec((tm, tn), lambda j, i: (i, j)),
        compiler_params=_cparams(("parallel", "parallel"), 48),
        name="matmul",
    )(xb, w)


def _attn_kernel(q_ref, k_ref, v_ref, o_ref, m_ref, acc_ref, *, tk, scale):
    bq = q_ref.shape[0]
    t = k_ref.shape[0]
    c = scale * math.log2(math.e)
    m_ref[...] = jnp.full(m_ref.shape, -jnp.inf, F32)
    acc_ref[...] = jnp.zeros(acc_ref.shape, F32)

    def body(ci, carry):
        start = pl.multiple_of(ci * tk, tk)
        k = k_ref[pl.ds(start, tk), :]
        v = v_ref[pl.ds(start, tk), :]
        for h in range(ATT_GROUP):
            r = slice(h * bq, (h + 1) * bq)
            s = lax.dot_general(q_ref[:, h * LANES:(h + 1) * LANES], k, (((1,), (1,)), ((), ())),
                                preferred_element_type=F32)
            m_prev = m_ref[r, :]
            m_new = jnp.maximum(m_prev, jnp.max(s, axis=1, keepdims=True))
            p = jnp.exp2((s - jnp.tile(m_new, (1, tk // LANES))) * c)
            alpha = jnp.exp2((m_prev - m_new) * c)
            pv = jnp.dot(p.astype(BF16), v, preferred_element_type=F32)
            acc_ref[r, :] = jnp.tile(alpha, (1, 2)) * acc_ref[r, :] + pv
            m_ref[r, :] = m_new
        return carry

    lax.fori_loop(0, t // tk, body, 0, unroll=2)
    for h in range(ATT_GROUP):
        r = slice(h * bq, (h + 1) * bq)
        o_ref[:, h * LANES:(h + 1) * LANES] = (acc_ref[r, :LANES] / acc_ref[r, LANES:]).astype(o_ref.dtype)


def _attention(qkv, tok0, nseq, t, bq, tk):
    assert t % bq == 0 and t % tk == 0 and tok0 % t == 0
    qb0 = tok0 // bq
    sb0 = tok0 // t
    nq = t // bq
    gw = ATT_GROUP * LANES
    v_blk0 = (ATT_HEADS + ATT_KV_HEADS) // 2
    return pl.pallas_call(
        functools.partial(_attn_kernel, tk=tk, scale=ATT_HEAD_DIM ** -0.5),
        out_shape=jax.ShapeDtypeStruct((nseq * t, ATT_WIDTH), BF16),
        grid=(nseq, ATT_KV_HEADS, nq),
        in_specs=[pl.BlockSpec((bq, gw), lambda b, g, i: (qb0 + b * nq + i, g)),
                  pl.BlockSpec((t, LANES), lambda b, g, i: (sb0 + b, ATT_HEADS + g)),
                  pl.BlockSpec((t, 2 * LANES), lambda b, g, i: (sb0 + b, v_blk0 + g))],
        out_specs=pl.BlockSpec((bq, gw), lambda b, g, i: (b * nq + i, g)),
        scratch_shapes=[pltpu.VMEM((ATT_GROUP * bq, LANES), F32),
                        pltpu.VMEM((ATT_GROUP * bq, 2 * LANES), F32)],
        compiler_params=_cparams(("parallel", "parallel", "parallel"), 40),
        name="attention",
    )(qkv, qkv, qkv)


def _mlstm_chains(chains):
    C = range(len(chains))
    ds = [c[0] for c in chains]
    q, k, kt, v, gt = ([c[i] for c in chains] for i in (2, 3, 4, 5, 6))
    h_ref, c_ref, n_ref, m_ref = ([c[i] for c in chains] for i in (7, 8, 9, 10))
    L, dk = q[0].shape
    ci = [c[0] * ML_HEADS + c[1] for c in chains]
    ig_row = [gt[i][ci[i]:ci[i] + 1, :] for i in C]
    logf_row = [_log_sigmoid(gt[i][2 * ML_HEADS + ci[i]:2 * ML_HEADS + ci[i] + 1, :]) for i in C]

    row = lax.broadcasted_iota(I32, (L, L), 0)
    col = lax.broadcasted_iota(I32, (L, L), 1)
    masks = (col <= row, col >= row)
    mask = [masks[ds[i]] for i in C]
    b_col = [jnp.sum(jnp.where(mask[i], logf_row[i], 0.0), axis=1, keepdims=True) for i in C]
    b_row = [jnp.sum(jnp.where(row == col, b_col[i], 0.0), axis=0, keepdims=True) for i in C]
    gtot = [jnp.sum(logf_row[i], axis=1, keepdims=True) for i in C]

    m_prev = [m_ref[i][...] for i in C]
    scale = dk ** -0.5
    assert math.log2(dk) % 2 == 0

    a_row = [gtot[i] - b_row[i] + ig_row[i] for i in C]
    m_loc = [jnp.max(a_row[i], axis=1, keepdims=True) for i in C]
    w_row = [jnp.exp(a_row[i] - m_loc[i]) * scale for i in C]
    c_loc = [jnp.dot((kt[i].astype(F32) * w_row[i]).astype(BF16), v[i], preferred_element_type=F32)
             for i in C]
    n_loc = [jnp.dot(jnp.broadcast_to(w_row[i], (8, L)).astype(BF16), k[i],
                     preferred_element_type=F32)[0:1, :] for i in C]

    dmat = [jnp.where(mask[i], b_col[i] - b_row[i] + ig_row[i], -jnp.inf) for i in C]
    inter = [b_col[i] + m_prev[i] for i in C]
    m_j = [jnp.maximum(jnp.max(dmat[i], axis=1, keepdims=True), inter[i]) for i in C]
    qk = [jnp.dot(q[i], kt[i], preferred_element_type=F32) for i in C]
    s = [qk[i] * (jnp.exp(dmat[i] - m_j[i]) * scale) for i in C]
    s_int = [jnp.exp(inter[i] - m_j[i]) for i in C]
    qc = [jnp.dot(q[i], c_ref[i][...].astype(BF16), preferred_element_type=F32) for i in C]
    num = [jnp.dot(s[i].astype(BF16), v[i], preferred_element_type=F32) + s_int[i] * qc[i] for i in C]
    den = [jnp.sum(s[i], axis=1, keepdims=True)
           + s_int[i] * jnp.sum(q[i].astype(F32) * n_ref[i][...], axis=1, keepdims=True) for i in C]
    for i in C:
        h_ref[i][...] = num[i] / jnp.maximum(jnp.abs(den[i]), jnp.exp(-m_j[i]))

    m_new = [jnp.maximum(gtot[i] + m_prev[i], m_loc[i]) for i in C]
    s_old = [jnp.exp(gtot[i] + m_prev[i] - m_new[i]) for i in C]
    s_new = [jnp.exp(m_loc[i] - m_new[i]) for i in C]
    for i in C:
        c_ref[i][...] = s_old[i] * c_ref[i][...] + s_new[i] * c_loc[i]
        n_ref[i][...] = s_old[i] * n_ref[i][...] + s_new[i] * n_loc[i]
        m_ref[i][...] = m_new[i]


def _mlstm_kernel(first_ref, qf_ref, kf_ref, vf_ref, ktf_ref, gtf_ref, qb_ref, kb_ref, vb_ref, ktb_ref,
                  gtb_ref, hf_ref, hb_ref, c_ref, n_ref, m_ref):
    i = pl.program_id(0)
    L = ML_CHUNK
    n_sub = qf_ref.shape[0] // L
    dirs = ((qf_ref, kf_ref, vf_ref, ktf_ref, gtf_ref, hf_ref),
            (qb_ref, kb_ref, vb_ref, ktb_ref, gtb_ref, hb_ref))
    for d in range(2):
        @pl.when(first_ref[d, i] == 1)
        def _():
            for hd in range(ML_HEADS):
                s = d * ML_HEADS + hd
                c_ref[s] = jnp.zeros(c_ref.shape[1:], F32)
                n_ref[s] = jnp.zeros(n_ref.shape[1:], F32)
                m_ref[s] = jnp.zeros(m_ref.shape[1:], F32)

    for sub in range(n_sub):
        for d, (q_ref, k_ref, v_ref, kt_ref, gt_ref, h_ref) in enumerate(dirs):
            rows = pl.ds((sub if d == 0 else n_sub - 1 - sub) * L, L)
            gt = gt_ref[:, rows]
            chains = []
            for hd in range(ML_HEADS):
                s = d * ML_HEADS + hd
                cols = slice(hd * ML_HEAD_DIM, (hd + 1) * ML_HEAD_DIM)
                chains.append((d, hd, q_ref[rows, cols], k_ref[rows, cols], kt_ref[cols, rows],
                               v_ref[rows, cols], gt, h_ref.at[rows, cols], c_ref.at[s], n_ref.at[s],
                               m_ref.at[s]))
            for g in range(0, ML_HEADS, ML_INTERLEAVE):
                _mlstm_chains(chains[g:g + ML_INTERLEAVE])


def _mlstm(rest, k_t, gates_t, seqs, n_sub):
    n = rest.shape[0]
    L = n_sub * ML_CHUNK
    nc = n // L
    first_f = np.zeros((nc,), np.int32)
    last_f = np.zeros((nc,), np.int32)
    tok = 0
    for (b, t) in seqs:
        assert t % L == 0
        for _ in range(b):
            first_f[tok // L] = 1
            last_f[(tok + t) // L - 1] = 1
            tok += t
    first = np.stack([first_f, last_f[::-1]]).astype(np.int32)
    W = ML_WIDTH
    nstate = 2 * ML_HEADS

    def fwd(c):
        return lambda i, fst: (i, c)

    def bwd(c):
        return lambda i, fst: (nc - 1 - i, c)

    def specs(m, mt):
        return [pl.BlockSpec((L, W), m(0)), pl.BlockSpec((L, W), m(1)), pl.BlockSpec((L, W), m(2)),
                pl.BlockSpec((W, L), mt), pl.BlockSpec((N_ML_GATES, L), mt)]

    grid_spec = pltpu.PrefetchScalarGridSpec(
        num_scalar_prefetch=1,
        grid=(nc,),
        in_specs=(specs(fwd, lambda i, fst: (0, i)) + specs(bwd, lambda i, fst: (0, nc - 1 - i))),
        out_specs=(pl.BlockSpec((L, W), fwd(0)), pl.BlockSpec((L, W), bwd(0))),
        scratch_shapes=[pltpu.VMEM((nstate, ML_HEAD_DIM, ML_HEAD_DIM), F32),
                        pltpu.VMEM((nstate, 1, ML_HEAD_DIM), F32),
                        pltpu.VMEM((nstate, 1, 1), F32)],
    )
    return pl.pallas_call(
        _mlstm_kernel,
        out_shape=(jax.ShapeDtypeStruct((n, W), F32), jax.ShapeDtypeStruct((n, W), F32)),
        grid_spec=grid_spec,
        compiler_params=_cparams(("arbitrary",), 32),
        name="mlstm",
    )(jnp.asarray(first), rest, rest, rest, k_t, gates_t, rest, rest, rest, k_t, gates_t)


def _merge_kernel(a_ref, hf_ref, hb_ref, mo_ref, ga_ref, gm_ref, wa_ref, wm_ref, wo_ref,
                  lng_ref, lnb_ref, *refs, alpha, split_block):
    x_refs, (xo_ref, xob_ref) = refs[:-2], refs[-2:]
    if split_block is None:
        x = x_refs[0][...]
    else:
        x = jnp.where(pl.program_id(0) < split_block, x_refs[0][...], x_refs[1][...])
    m_out = (_sigmoid(mo_ref[...].astype(F32)) * (hf_ref[...] + hb_ref[...])).astype(BF16)
    pa = jnp.dot(a_ref[...], wa_ref[...], preferred_element_type=F32)
    pm = jnp.dot(m_out, wm_ref[...], preferred_element_type=F32)
    merged = _sigmoid(ga_ref[...].astype(F32)) * pa + _sigmoid(gm_ref[...].astype(F32)) * pm
    y = jnp.dot(merged.astype(BF16), wo_ref[...], preferred_element_type=F32)
    z = _layer_norm(alpha * x + y, lng_ref[...], lnb_ref[...])
    xo_ref[...] = z
    xob_ref[...] = z.astype(BF16)


def _merge(a_out, hf, hb, rest, xs, wa, wm, wo, lng, lnb, alpha, tm):
    n, d = a_out.shape[0], wo.shape[1]
    assert n % tm == 0 and d == 2 * ML_WIDTH and ATT_WIDTH == ML_WIDTH
    const = dict(pipeline_mode=pl.Buffered(1))
    if len(xs) == 1:
        sb = None
        x_specs = [pl.BlockSpec((tm, d), lambda i: (i, 0))]
    else:
        assert len(xs) == 2 and xs[0].shape[0] % tm == 0 and xs[0].shape[0] + xs[1].shape[0] == n
        sb = xs[0].shape[0] // tm
        x_specs = [pl.BlockSpec((tm, d), lambda i: (jnp.minimum(i, sb - 1), 0)),
                   pl.BlockSpec((tm, d), lambda i: (jnp.maximum(i - sb, 0), 0))]
    return pl.pallas_call(
        functools.partial(_merge_kernel, alpha=alpha, split_block=sb),
        out_shape=(jax.ShapeDtypeStruct((n, d), F32), jax.ShapeDtypeStruct((n, d), BF16)),
        grid=(n // tm,),
        in_specs=[pl.BlockSpec((tm, ATT_WIDTH), lambda i: (i, 0)),
                  pl.BlockSpec((tm, ML_WIDTH), lambda i: (i, 0)),
                  pl.BlockSpec((tm, ML_WIDTH), lambda i: (i, 0)),
                  pl.BlockSpec((tm, ML_WIDTH), lambda i: (i, 3)),
                  pl.BlockSpec((tm, d), lambda i: (i, 2)),
                  pl.BlockSpec((tm, d), lambda i: (i, 3)),
                  pl.BlockSpec((ATT_WIDTH, d), lambda i: (0, 0), **const),
                  pl.BlockSpec((ML_WIDTH, d), lambda i: (0, 0), **const),
                  pl.BlockSpec((d, d), lambda i: (0, 0), **const),
                  pl.BlockSpec((1, d), lambda i: (0, 0)),
                  pl.BlockSpec((1, d), lambda i: (0, 0)),
                  *x_specs],
        out_specs=(pl.BlockSpec((tm, d), lambda i: (i, 0)),
                   pl.BlockSpec((tm, d), lambda i: (i, 0))),
        compiler_params=_cparams(("parallel",), 56),
        name="merge",
    )(a_out, hf, hb, rest, rest, rest, wa, wm, wo, lng, lnb, *xs)


def _ffn_kernel(xb_ref, w1_ref, w3_ref, w2_ref, x_ref, lng_ref, lnb_ref, *refs, alpha, n_cast):
    cast_in = refs[:n_cast]
    xo_ref, xob_ref = refs[n_cast:n_cast + 2]
    cast_out = refs[n_cast + 2:2 * n_cast + 2]
    acc_ref = refs[-1]
    j = pl.program_id(1)

    for src, dst in zip(cast_in, cast_out):
        dst[...] = src[...].astype(BF16)

    @pl.when(j == 0)
    def _():
        acc_ref[...] = jnp.zeros(acc_ref.shape, F32)

    xb = xb_ref[...]
    a = jnp.dot(xb, w1_ref[...], preferred_element_type=F32)
    b = jnp.dot(xb, w3_ref[...], preferred_element_type=F32)
    hid = (a * _sigmoid(a) * b).astype(BF16)
    acc_ref[...] += jnp.dot(hid, w2_ref[...], preferred_element_type=F32)

    @pl.when(j == pl.num_programs(1) - 1)
    def _():
        z = _layer_norm(alpha * x_ref[...] + acc_ref[...], lng_ref[...], lnb_ref[...])
        xo_ref[...] = z
        xob_ref[...] = z.astype(BF16)


def _cast_block_rows(rows, steps):
    rb = 16 * pl.cdiv(pl.cdiv(rows, steps), 16)
    while rows % rb:
        rb += 16
    return rb


def _ffn(xb, x, w1, w3, w2, lng, lnb, alpha, tm, tf, to_cast=()):
    n, d = x.shape
    f = w1.shape[1]
    assert n % tm == 0 and f % tf == 0
    nj = f // tf
    steps = (n // tm) * nj
    cast_specs = []
    for a in to_cast:
        rb = _cast_block_rows(a.shape[0], steps)
        last = a.shape[0] // rb - 1
        cast_specs.append(pl.BlockSpec((rb, a.shape[1]),
                                       lambda i, j, last=last: (jnp.minimum(i * nj + j, last), 0)))
    outs = pl.pallas_call(
        functools.partial(_ffn_kernel, alpha=alpha, n_cast=len(to_cast)),
        out_shape=(jax.ShapeDtypeStruct((n, d), F32), jax.ShapeDtypeStruct((n, d), BF16),
                   *[jax.ShapeDtypeStruct(a.shape, BF16) for a in to_cast]),
        grid=(n // tm, nj),
        in_specs=[pl.BlockSpec((tm, d), lambda i, j: (i, 0)),
                  pl.BlockSpec((d, tf), lambda i, j: (0, j)),
                  pl.BlockSpec((d, tf), lambda i, j: (0, j)),
                  pl.BlockSpec((tf, d), lambda i, j: (j, 0)),
                  pl.BlockSpec((tm, d), lambda i, j: (i, 0)),
                  pl.BlockSpec((1, d), lambda i, j: (0, 0)),
                  pl.BlockSpec((1, d), lambda i, j: (0, 0)),
                  *cast_specs],
        out_specs=(pl.BlockSpec((tm, d), lambda i, j: (i, 0)),
                   pl.BlockSpec((tm, d), lambda i, j: (i, 0)),
                   *cast_specs),
        scratch_shapes=[pltpu.VMEM((tm, d), F32)],
        compiler_params=_cparams(("arbitrary", "arbitrary"), 56),
        name="ffn_dense",
    )(xb, w1, w3, w2, x, lng, lnb, *to_cast)
    return outs[0], outs[1], outs[2:]


def _router_kernel(x_ref, wr_ref, tri_ref, ints_ref, flts_ref, cnt_ref, base_ref):
    i = pl.program_id(0)

    @pl.when(i == 0)
    def _():
        base_ref[...] = jnp.zeros(base_ref.shape, F32)

    logits = jnp.dot(x_ref[...], wr_ref[...], preferred_element_type=F32,
                     precision=lax.Precision.HIGHEST)
    lane = lax.broadcasted_iota(I32, logits.shape, 1)
    lanef = lane.astype(F32)
    logits = jnp.where(lane < N_EXPERTS, logits, -jnp.inf)
    v0 = jnp.max(logits, axis=1, keepdims=True)
    i0 = jnp.min(jnp.where(logits == v0, lanef, float(LANES)), axis=1, keepdims=True)
    rest = jnp.where(lanef == i0, -jnp.inf, logits)
    v1 = jnp.max(rest, axis=1, keepdims=True)
    i1 = jnp.min(jnp.where(rest == v1, lanef, float(LANES)), axis=1, keepdims=True)
    e1 = jnp.exp(v1 - v0)
    g0 = 1.0 / (1.0 + e1)
    g1 = e1 / (1.0 + e1)

    sel0 = lanef == i0
    sel1 = lanef == i1
    onehot = jnp.where(sel0 | sel1, 1.0, 0.0)
    before = jnp.dot(tri_ref[...], onehot.astype(BF16), preferred_element_type=F32) + base_ref[...]
    r0 = jnp.sum(jnp.where(sel0, before, 0.0), axis=1, keepdims=True)
    r1 = jnp.sum(jnp.where(sel1, before, 0.0), axis=1, keepdims=True)
    base_ref[...] += jnp.sum(onehot, axis=0, keepdims=True)
    cnt_ref[...] = jnp.broadcast_to(base_ref[...], cnt_ref.shape)

    ints = jnp.where(lane == 0, i0, jnp.where(lane == 1, i1, jnp.where(lane == 2, r0,
                     jnp.where(lane == 3, r1, 0.0))))
    ints_ref[...] = ints.astype(I32)
    flts_ref[...] = jnp.where(lane == 0, g0, jnp.where(lane == 1, g1, 0.0))


def _router(x, wr, tr):
    n, d = x.shape
    assert n % tr == 0
    tri = jnp.asarray(np.tril(np.ones((tr, tr), np.float32), -1), BF16)
    return pl.pallas_call(
        _router_kernel,
        out_shape=(jax.ShapeDtypeStruct((n, LANES), I32),
                   jax.ShapeDtypeStruct((n, LANES), F32),
                   jax.ShapeDtypeStruct((8, LANES), F32)),
        grid=(n // tr,),
        in_specs=[pl.BlockSpec((tr, d), lambda i: (i, 0)),
                  pl.BlockSpec((d, LANES), lambda i: (0, 0)),
                  pl.BlockSpec((tr, tr), lambda i: (0, 0))],
        out_specs=(pl.BlockSpec((tr, LANES), lambda i: (i, 0)),
                   pl.BlockSpec((tr, LANES), lambda i: (i, 0)),
                   pl.BlockSpec((8, LANES), lambda i: (0, 0))),
        scratch_shapes=[pltpu.VMEM((1, LANES), F32)],
        compiler_params=_cparams(("arbitrary",), 32),
        name="moe_router",
    )(x, wr, tri)


def _row_dma_start(make_copy, lo, hi):
    def issue(t, carry):
        for k in range(TOP_K):
            make_copy(t, k).start()
        return carry

    lax.fori_loop(lo, hi, issue, 0)


def _row_dma_wait(make_copy, lo, hi):
    def drain(t, carry):
        for k in range(TOP_K):
            make_copy(t, k).wait()
        return carry

    lax.fori_loop(lo, hi, drain, 0)


def _dispatch_kernel(pos_ref, x_ref, xs_in_ref, xs_ref, sem):
    del xs_in_ref

    def copy(t, k):
        return pltpu.make_async_copy(x_ref.at[pl.ds(t, 1), :],
                                     xs_ref.at[pl.ds(pos_ref[0, k, t], 1), :], sem)

    _row_dma_start(copy, 0, x_ref.shape[0])
    _row_dma_wait(copy, 0, x_ref.shape[0])


def _dispatch(x, pos, n_rows, tt):
    n, d = x.shape
    xs0 = jnp.zeros((n_rows, d), x.dtype)
    return pl.pallas_call(
        _dispatch_kernel,
        out_shape=jax.ShapeDtypeStruct(xs0.shape, xs0.dtype),
        grid=(n // tt,),
        in_specs=[pl.BlockSpec((1, TOP_K, tt), lambda i: (i, 0, 0), memory_space=pltpu.SMEM),
                  pl.BlockSpec((tt, d), lambda i: (i, 0)),
                  pl.BlockSpec(memory_space=pl.ANY)],
        out_specs=pl.BlockSpec(memory_space=pl.ANY),
        scratch_shapes=[pltpu.SemaphoreType.DMA],
        input_output_aliases={2: 0},
        compiler_params=_cparams(("arbitrary",), 32),
        name="moe_dispatch",
    )(pos, x, xs0)


def _moe_ffn_kernel(te_ref, nv_ref, xs_ref, w1_ref, w3_ref, w2_ref, y_ref, xb_ref):
    i = pl.program_id(0)
    j = pl.program_id(1)

    @pl.when(i < nv_ref[0])
    def _():
        @pl.when(j == 0)
        def _():
            xb_ref[...] = xs_ref[...].astype(BF16)
            y_ref[...] = jnp.zeros(y_ref.shape, F32)

        xb = xb_ref[...]
        a = jnp.dot(xb, w1_ref[...], preferred_element_type=F32)
        b = jnp.dot(xb, w3_ref[...], preferred_element_type=F32)
        hid = (a * _sigmoid(a) * b).astype(BF16)
        y_ref[...] += jnp.dot(hid, w2_ref[...], preferred_element_type=F32)

    @pl.when((i >= nv_ref[0]) & (j == 0))
    def _():
        y_ref[...] = jnp.zeros(y_ref.shape, F32)


def _moe_ffn(xs, tile_expert, n_valid, w1, w3, w2, tm, tf):
    p, d = xs.shape
    f = w1.shape[2]
    assert p % tm == 0 and f % tf == 0
    nj = f // tf

    def row_map(i, j, te, nv):
        return (jnp.minimum(i, nv[0] - 1), 0)

    def jj(i, j, nv):
        return jnp.where(i < nv[0], j, nj - 1)

    grid_spec = pltpu.PrefetchScalarGridSpec(
        num_scalar_prefetch=2,
        grid=(p // tm, nj),
        in_specs=[pl.BlockSpec((tm, d), row_map),
                  pl.BlockSpec((None, d, tf), lambda i, j, te, nv: (te[i], 0, jj(i, j, nv))),
                  pl.BlockSpec((None, d, tf), lambda i, j, te, nv: (te[i], 0, jj(i, j, nv))),
                  pl.BlockSpec((None, tf, d), lambda i, j, te, nv: (te[i], jj(i, j, nv), 0))],
        out_specs=pl.BlockSpec((tm, d), lambda i, j, te, nv: (i, 0)),
        scratch_shapes=[pltpu.VMEM((tm, d), BF16)],
    )
    return pl.pallas_call(
        _moe_ffn_kernel,
        out_shape=jax.ShapeDtypeStruct((p, d), F32),
        grid_spec=grid_spec,
        compiler_params=_cparams(("arbitrary", "arbitrary"), 56),
        name="moe_ffn",
    )(tile_expert, n_valid, xs, w1, w3, w2)


def _combine_kernel(pos_ref, y_ref, g_ref, x_ref, lng_ref, lnb_ref, *refs, alpha, split_block):
    out_refs = refs[:-3]
    buf0_ref, buf1_ref, sems = refs[-3:]
    tt = x_ref.shape[0]
    n_part = sems.shape[0]
    part = tt // n_part

    def copy(p):
        def make(t, k):
            buf = buf0_ref if k == 0 else buf1_ref
            return pltpu.make_async_copy(y_ref.at[pl.ds(pos_ref[0, k, t], 1), :], buf.at[pl.ds(t, 1), :],
                                         sems.at[p])
        return make

    for p in range(n_part):
        _row_dma_start(copy(p), p * part, (p + 1) * part)
    for p in range(n_part):
        _row_dma_wait(copy(p), p * part, (p + 1) * part)
        r = slice(p * part, (p + 1) * part)
        f = g_ref[r, 0:1] * buf0_ref[r, :] + g_ref[r, 1:2] * buf1_ref[r, :]
        z = _layer_norm(alpha * x_ref[r, :] + f, lng_ref[...], lnb_ref[...])
        if split_block is None:
            out_refs[0][r, :] = z
        else:
            @pl.when(pl.program_id(0) < split_block)
            def _():
                out_refs[0][r, :] = z

            @pl.when(pl.program_id(0) >= split_block)
            def _():
                out_refs[1][r, :] = z


def _combine(y, pos, gates, x, lng, lnb, alpha, tt, split=None):
    n, d = x.shape
    if split is None:
        sb = None
        out_shape = jax.ShapeDtypeStruct((n, d), F32)
        out_specs = pl.BlockSpec((tt, d), lambda i: (i, 0))
    else:
        assert split % tt == 0 and 0 < split < n
        sb = split // tt
        out_shape = (jax.ShapeDtypeStruct((split, d), F32), jax.ShapeDtypeStruct((n - split, d), F32))
        out_specs = (pl.BlockSpec((tt, d), lambda i: (jnp.minimum(i, sb - 1), 0)),
                     pl.BlockSpec((tt, d), lambda i: (jnp.maximum(i - sb, 0), 0)))
    return pl.pallas_call(
        functools.partial(_combine_kernel, alpha=alpha, split_block=sb),
        out_shape=out_shape,
        grid=(n // tt,),
        in_specs=[pl.BlockSpec((1, TOP_K, tt), lambda i: (i, 0, 0), memory_space=pltpu.SMEM),
                  pl.BlockSpec(memory_space=pl.ANY),
                  pl.BlockSpec((tt, LANES), lambda i: (i, 0)),
                  pl.BlockSpec((tt, d), lambda i: (i, 0)),
                  pl.BlockSpec((1, d), lambda i: (0, 0)),
                  pl.BlockSpec((1, d), lambda i: (0, 0))],
        out_specs=out_specs,
        scratch_shapes=[pltpu.VMEM((tt, d), F32), pltpu.VMEM((tt, d), F32),
                        pltpu.SemaphoreType.DMA((COMBINE_PARTS,))],
        compiler_params=_cparams(("arbitrary",), 48),
        name="moe_combine",
    )(pos, y, gates, x, lng, lnb)


def _moe(x, wr, w1, w3, w2, lng, lnb, alpha, tm, tf, tr, tt, split=None):
    n, d = x.shape
    ints, flts, cnt = _router(x, wr, tr)
    e0, e1, r0, r1 = ints[:, 0], ints[:, 1], ints[:, 2], ints[:, 3]
    counts = cnt[0, :N_EXPERTS].astype(I32)

    tiles = (counts + tm - 1) // tm
    tile_end = jnp.cumsum(tiles)
    offs = (tile_end - tiles) * tm
    n_tiles = (TOP_K * n) // tm + N_EXPERTS
    n_valid = tile_end[-1:]
    tile_expert = jnp.sum(jnp.arange(n_tiles, dtype=I32)[:, None] >= tile_end[None, :], axis=1)
    tile_expert = jnp.minimum(tile_expert, N_EXPERTS - 1).astype(I32)
    tile_expert = jnp.where(jnp.arange(n_tiles) < n_valid[0], tile_expert,
                            tile_expert[jnp.maximum(n_valid[0] - 1, 0)])
    pos = jnp.stack([offs[e0] + r0, offs[e1] + r1], axis=0)
    pos = pos.reshape(TOP_K, n // tt, tt).transpose(1, 0, 2)

    xs = _dispatch(x, pos, n_tiles * tm, tt)
    y = _moe_ffn(xs, tile_expert, n_valid.astype(I32), w1, w3, w2, tm, tf)
    return _combine(y, pos, flts, x, lng[None, :], lnb[None, :], alpha, tt, split=split)


def _rope_tables(t_max):
    half = ATT_HEAD_DIM // 2
    pos = np.arange(t_max)
    pos_r = (pos // GRID_W).astype(np.float32)
    pos_c = (pos % GRID_W).astype(np.float32)
    inv = jnp.asarray(ROPE_THETA, F32) ** (-jnp.arange(0, half, 2, dtype=F32) / half)
    ang_r = jnp.asarray(pos_r)[:, None] * inv
    ang_c = jnp.asarray(pos_c)[:, None] * inv
    cr, sr, cc, sc = jnp.cos(ang_r), jnp.sin(ang_r), jnp.cos(ang_c), jnp.sin(ang_c)
    zero = jnp.zeros_like(sr)
    cos = jnp.concatenate([cr, cr, cc, cc], axis=1)
    sin_up = jnp.concatenate([-sr, zero, -sc, zero], axis=1)
    sin_dn = jnp.concatenate([zero, sr, zero, sc], axis=1)
    return cos, sin_up, sin_dn


def _split_w_in(w_in_l):
    sizes = (ATT_WIDTH, KV_WIDTH, KV_WIDTH, ML_WIDTH, ML_WIDTH, ML_WIDTH, ML_WIDTH, N_ML_GATES)
    o = np.cumsum((0,) + sizes)
    d = w_in_l.shape[0]
    wb = lax.optimization_barrier(w_in_l.astype(BF16))
    w_gate = jnp.concatenate([wb[:, o[7]:o[8]], jnp.zeros((d, LANES - N_ML_GATES), BF16)], axis=1)
    w_att = jnp.concatenate([wb[:, o[0]:o[3]], w_gate], axis=1)
    w_rest = jnp.concatenate([wb[:, o[3]:o[7]], wb[:, o[8]:]], axis=1)
    return w_att, w_rest


def kernel(x_prompt, x_sample, w_in, b_gates, q_gain, k_gain, w_att_br, w_ml_br, w_o, ln1_g, ln1_b,
           w1_d, w3_d, w2_d, router, e_w1, e_w3, e_w2, ln2_g, ln2_b):
    depth = w_in.shape[0]
    d = x_prompt.shape[-1]
    seqs = ((x_prompt.shape[0], x_prompt.shape[1]), (x_sample.shape[0], x_sample.shape[1]))
    n0 = seqs[0][0] * seqs[0][1]
    alpha = float((2 * depth) ** 0.25)

    x = [x_prompt.reshape(-1, d), x_sample.reshape(-1, d)]
    xb = jnp.concatenate([x[0].astype(BF16), x[1].astype(BF16)], axis=0)
    cos, sin_up, sin_dn = _rope_tables(max(seqs[0][1], seqs[1][1]))

    for l in range(depth):
        w_att, w_rest = _split_w_in(w_in[l])
        bias = jnp.concatenate([b_gates[l], jnp.zeros((LANES - N_ML_GATES,), F32)])[None, :]
        gain = jnp.concatenate([jnp.broadcast_to(q_gain[l], (ATT_HEADS, ATT_HEAD_DIM)),
                                jnp.broadcast_to(k_gain[l], (ATT_KV_HEADS, ATT_HEAD_DIM)),
                                jnp.zeros((16 - ATT_HEADS - ATT_KV_HEADS, ATT_HEAD_DIM), F32)], axis=0)

        T = TILES
        qkv, gates = _att_proj(xb, w_att, bias, gain, cos, sin_up, sin_dn, seqs, tm=T["att_proj_tm"])
        rest = _matmul(xb, w_rest, BF16, tm=T["proj_tm"], tn=T["proj_tn"])

        a_out = jnp.concatenate(
            [_attention(qkv, 0, seqs[0][0], seqs[0][1], bq=min(T["attn_bq"], seqs[0][1]), tk=T["attn_tk"]),
             _attention(qkv, n0, seqs[1][0], seqs[1][1], bq=min(T["attn_bq"], seqs[1][1]), tk=T["attn_tk"])],
            axis=0)
        hf, hb = _mlstm(rest, rest[:, ML_WIDTH:2 * ML_WIDTH].T, gates[:, :N_ML_GATES].T, seqs,
                        n_sub=T["mlstm_chunks"])

        x, xb = _merge(a_out, hf, hb, rest, x if l == 0 else [x], w_att_br[l].astype(BF16), w_ml_br[l].astype(BF16),
                       w_o[l].astype(BF16), ln1_g[l][None, :], ln1_b[l][None, :], alpha, tm=T["merge_tm"])

        j = l // 2
        if l % 2 == 0:
            nxt = (e_w1[j], e_w3[j], e_w2[j]) if l + 1 < depth else ()
            x, xb, cast = _ffn(xb, x, w1_d[j].astype(BF16), w3_d[j].astype(BF16), w2_d[j].astype(BF16),
                               ln2_g[l][None, :], ln2_b[l][None, :], alpha, tm=T["ffn_tm"], tf=T["ffn_tf"],
                               to_cast=[e.reshape(-1, e.shape[-1]) for e in nxt])
            expert_w = [c.reshape(e.shape) for c, e in zip(cast, nxt)]
        else:
            wr = jnp.concatenate([router[j], jnp.zeros((d, LANES - N_EXPERTS), F32)], axis=1)
            last = l == depth - 1
            x = _moe(x, wr, *expert_w, ln2_g[l], ln2_b[l], alpha, tm=T["moe_tm"], tf=T["moe_tf"],
                     tr=T["router_tr"], tt=T["moe_tt"], split=n0 if last else None)
            if last:
                return (x[0].reshape(x_prompt.shape), x[1].reshape(x_sample.shape))
            xb = x.astype(BF16)

    return (x[:n0].reshape(x_prompt.shape), x[n0:].reshape(x_sample.shape))
```

```python
import functools
import math

import numpy as np
import jax
import jax.numpy as jnp
from jax import lax
from jax.experimental import pallas as pl
from jax.experimental.pallas import tpu as pltpu

F32 = jnp.float32
BF16 = jnp.bfloat16
I32 = jnp.int32

GRID_W = 64
ATT_HEADS = 8
ATT_KV_HEADS = 2
ATT_HEAD_DIM = 128
ATT_GROUP = ATT_HEADS // ATT_KV_HEADS
ROPE_THETA = 10000.0
QK_EPS = 1e-6
ML_HEADS = 4
ML_HEAD_DIM = 256
ML_CHUNK = 128
ML_INTERLEAVE = 2
COMBINE_PARTS = 2
ATT_WIDTH = ATT_HEADS * ATT_HEAD_DIM
KV_WIDTH = ATT_KV_HEADS * ATT_HEAD_DIM
ML_WIDTH = ML_HEADS * ML_HEAD_DIM
N_ML_GATES = 4 * ML_HEADS
N_EXPERTS = 8
TOP_K = 2
LN_EPS = 1e-5

LANES = 128
V7X_VMEM_BYTES = 64 * 2 ** 20
MIB = 2 ** 20

TILES = dict(att_proj_tm=512, proj_tm=1024, proj_tn=1024, attn_bq=1024, attn_tk=512, merge_tm=256,
             ffn_tm=512, ffn_tf=512, moe_tm=512, moe_tf=1024, router_tr=512, moe_tt=512, mlstm_chunks=2)


def _cparams(semantics, vmem_mib):
    assert vmem_mib * MIB < V7X_VMEM_BYTES
    return pltpu.CompilerParams(dimension_semantics=semantics, vmem_limit_bytes=vmem_mib * MIB)


def _sigmoid(x):
    return 1.0 / (1.0 + jnp.exp(-x))


def _log_sigmoid(x):
    return jnp.minimum(x, 0.0) - jnp.log1p(jnp.exp(-jnp.abs(x)))


def _mean_last(z, n_axes):
    count = 1
    for ax in range(z.ndim - 1, z.ndim - 1 - n_axes, -1):
        count *= z.shape[ax]
        z = jnp.sum(z, axis=ax, keepdims=True)
    return z * (1.0 / count)


def _layer_norm(z, g, b, n_axes=1):
    mu = _mean_last(z, n_axes)
    zc = z - mu
    var = _mean_last(zc * zc, n_axes)
    return zc * lax.rsqrt(var + LN_EPS) * g + b


def _att_proj_kernel(x_ref, w_ref, bias_ref, gain_ref, cos_ref, sa_ref, sb_ref,
                     qkv_ref, gates_ref, *, n_heads, n_norm_heads):
    x = x_ref[...]
    cos = cos_ref[...]
    sa = sa_ref[...]
    sb = sb_ref[...]
    ones = jnp.ones((x.shape[0], LANES), qkv_ref.dtype)
    for h in range(n_heads):
        if h % 2 == 0:
            pair = jnp.dot(x, w_ref[:, h * LANES:(h + 2) * LANES], preferred_element_type=F32)
        a = pair[:, (h % 2) * LANES:(h % 2 + 1) * LANES]
        if h < n_norm_heads:
            a = a * lax.rsqrt(jnp.mean(a * a, axis=-1, keepdims=True) + QK_EPS) * gain_ref[h:h + 1, :]
            a = a * cos + pltpu.roll(a, 96, 1) * sa + pltpu.roll(a, 32, 1) * sb
            qkv_ref[:, h * LANES:(h + 1) * LANES] = a.astype(qkv_ref.dtype)
        else:
            c = n_norm_heads + 2 * (h - n_norm_heads)
            qkv_ref[:, c * LANES:(c + 1) * LANES] = a.astype(qkv_ref.dtype)
            qkv_ref[:, (c + 1) * LANES:(c + 2) * LANES] = ones
    gates_ref[...] = (jnp.dot(x, w_ref[:, n_heads * LANES:], preferred_element_type=F32)
                      + bias_ref[...])


def _att_proj(xb, w, bias, gain, cos, sa, sb, seqs, tm):
    n, d = xb.shape
    n_heads = (ATT_WIDTH + 2 * KV_WIDTH) // LANES
    ncol = n_heads * LANES + LANES
    assert w.shape == (d, ncol)
    n_out = (n_heads + ATT_KV_HEADS) * LANES

    (b0, t0), (b1, t1) = seqs
    assert t0 % tm == 0 and t1 % tm == 0
    nb0 = b0 * t0 // tm

    def pos_map(i):
        return (jnp.where(i < nb0, i % (t0 // tm), (i - nb0) % (t1 // tm)), 0)

    return pl.pallas_call(
        functools.partial(_att_proj_kernel, n_heads=n_heads, n_norm_heads=ATT_HEADS + ATT_KV_HEADS),
        out_shape=(jax.ShapeDtypeStruct((n, n_out), BF16),
                   jax.ShapeDtypeStruct((n, LANES), F32)),
        grid=(n // tm,),
        in_specs=[pl.BlockSpec((tm, d), lambda i: (i, 0)),
                  pl.BlockSpec((d, ncol), lambda i: (0, 0)),
                  pl.BlockSpec((1, LANES), lambda i: (0, 0)),
                  pl.BlockSpec((16, LANES), lambda i: (0, 0)),
                  pl.BlockSpec((tm, LANES), pos_map),
                  pl.BlockSpec((tm, LANES), pos_map),
                  pl.BlockSpec((tm, LANES), pos_map)],
        out_specs=(pl.BlockSpec((tm, n_out), lambda i: (i, 0)),
                   pl.BlockSpec((tm, LANES), lambda i: (i, 0))),
        compiler_params=_cparams(("parallel",), 48),
        name="att_proj",
    )(xb, w, bias, gain, cos, sa, sb)


def _matmul_kernel(x_ref, w_ref, o_ref):
    o_ref[...] = jnp.dot(x_ref[...], w_ref[...], preferred_element_type=F32).astype(o_ref.dtype)


def _matmul(xb, w, out_dtype, tm, tn):
    n, d = xb.shape
    ncol = w.shape[1]
    assert n % tm == 0 and ncol % tn == 0
    return pl.pallas_call(
        _matmul_kernel,
        out_shape=jax.ShapeDtypeStruct((n, ncol), out_dtype),
        grid=(ncol // tn, n // tm),
        in_specs=[pl.BlockSpec((tm, d), lambda j, i: (i, 0)),
                  pl.BlockSpec((d, tn), lambda j, i: (0, j))],
        out_specs=pl.BlockSpec((tm, tn), lambda j, i: (i, j)),
        compiler_params=_cparams(("parallel", "parallel"), 48),
        name="matmul",
    )(xb, w)


def _attn_kernel(q_ref, k_ref, v_ref, o_ref, m_ref, acc_ref, *, tk, scale):
    bq = q_ref.shape[0]
    t = k_ref.shape[0]
    c = scale * math.log2(math.e)
    m_ref[...] = jnp.full(m_ref.shape, -jnp.inf, F32)
    acc_ref[...] = jnp.zeros(acc_ref.shape, F32)

    def body(ci, carry):
        start = pl.multiple_of(ci * tk, tk)
        k = k_ref[pl.ds(start, tk), :]
        v = v_ref[pl.ds(start, tk), :]
        for h in range(ATT_GROUP):
            r = slice(h * bq, (h + 1) * bq)
            s = lax.dot_general(q_ref[:, h * LANES:(h + 1) * LANES], k, (((1,), (1,)), ((), ())),
                                preferred_element_type=F32)
            m_prev = m_ref[r, :]
            m_new = jnp.maximum(m_prev, jnp.max(s, axis=1, keepdims=True))
            p = jnp.exp2((s - jnp.tile(m_new, (1, tk // LANES))) * c)
            alpha = jnp.exp2((m_prev - m_new) * c)
            pv = jnp.dot(p.astype(BF16), v, preferred_element_type=F32)
            acc_ref[r, :] = jnp.tile(alpha, (1, 2)) * acc_ref[r, :] + pv
            m_ref[r, :] = m_new
        return carry

    lax.fori_loop(0, t // tk, body, 0, unroll=2)
    for h in range(ATT_GROUP):
        r = slice(h * bq, (h + 1) * bq)
        o_ref[:, h * LANES:(h + 1) * LANES] = (acc_ref[r, :LANES] / acc_ref[r, LANES:]).astype(o_ref.dtype)


def _attention(qkv, tok0, nseq, t, bq, tk):
    assert t % bq == 0 and t % tk == 0 and tok0 % t == 0
    qb0 = tok0 // bq
    sb0 = tok0 // t
    nq = t // bq
    gw = ATT_GROUP * LANES
    v_blk0 = (ATT_HEADS + ATT_KV_HEADS) // 2
    return pl.pallas_call(
        functools.partial(_attn_kernel, tk=tk, scale=ATT_HEAD_DIM ** -0.5),
        out_shape=jax.ShapeDtypeStruct((nseq * t, ATT_WIDTH), BF16),
        grid=(nseq, ATT_KV_HEADS, nq),
        in_specs=[pl.BlockSpec((bq, gw), lambda b, g, i: (qb0 + b * nq + i, g)),
                  pl.BlockSpec((t, LANES), lambda b, g, i: (sb0 + b, ATT_HEADS + g)),
                  pl.BlockSpec((t, 2 * LANES), lambda b, g, i: (sb0 + b, v_blk0 + g))],
        out_specs=pl.BlockSpec((bq, gw), lambda b, g, i: (b * nq + i, g)),
        scratch_shapes=[pltpu.VMEM((ATT_GROUP * bq, LANES), F32),
                        pltpu.VMEM((ATT_GROUP * bq, 2 * LANES), F32)],
        compiler_params=_cparams(("parallel", "parallel", "parallel"), 40),
        name="attention",
    )(qkv, qkv, qkv)


def _mlstm_chains(chains):
    C = range(len(chains))
    ds = [c[0] for c in chains]
    q, k, kt, v, gt = ([c[i] for c in chains] for i in (2, 3, 4, 5, 6))
    h_ref, c_ref, n_ref, m_ref = ([c[i] for c in chains] for i in (7, 8, 9, 10))
    L, dk = q[0].shape
    ci = [c[0] * ML_HEADS + c[1] for c in chains]
    ig_row = [gt[i][ci[i]:ci[i] + 1, :] for i in C]
    logf_row = [_log_sigmoid(gt[i][2 * ML_HEADS + ci[i]:2 * ML_HEADS + ci[i] + 1, :]) for i in C]

    row = lax.broadcasted_iota(I32, (L, L), 0)
    col = lax.broadcasted_iota(I32, (L, L), 1)
    masks = (col <= row, col >= row)
    mask = [masks[ds[i]] for i in C]
    b_col = [jnp.sum(jnp.where(mask[i], logf_row[i], 0.0), axis=1, keepdims=True) for i in C]
    b_row = [jnp.sum(jnp.where(row == col, b_col[i], 0.0), axis=0, keepdims=True) for i in C]
    gtot = [jnp.sum(logf_row[i], axis=1, keepdims=True) for i in C]

    m_prev = [m_ref[i][...] for i in C]
    scale = dk ** -0.5
    assert math.log2(dk) % 2 == 0

    a_row = [gtot[i] - b_row[i] + ig_row[i] for i in C]
    m_loc = [jnp.max(a_row[i], axis=1, keepdims=True) for i in C]
    w_row = [jnp.exp(a_row[i] - m_loc[i]) * scale for i in C]
    c_loc = [jnp.dot((kt[i].astype(F32) * w_row[i]).astype(BF16), v[i], preferred_element_type=F32)
             for i in C]
    n_loc = [jnp.dot(jnp.broadcast_to(w_row[i], (8, L)).astype(BF16), k[i],
                     preferred_element_type=F32)[0:1, :] for i in C]

    dmat = [jnp.where(mask[i], b_col[i] - b_row[i] + ig_row[i], -jnp.inf) for i in C]
    inter = [b_col[i] + m_prev[i] for i in C]
    m_j = [jnp.maximum(jnp.max(dmat[i], axis=1, keepdims=True), inter[i]) for i in C]
    qk = [jnp.dot(q[i], kt[i], preferred_element_type=F32) for i in C]
    s = [qk[i] * (jnp.exp(dmat[i] - m_j[i]) * scale) for i in C]
    s_int = [jnp.exp(inter[i] - m_j[i]) for i in C]
    qc = [jnp.dot(q[i], c_ref[i][...].astype(BF16), preferred_element_type=F32) for i in C]
    num = [jnp.dot(s[i].astype(BF16), v[i], preferred_element_type=F32) + s_int[i] * qc[i] for i in C]
    den = [jnp.sum(s[i], axis=1, keepdims=True)
           + s_int[i] * jnp.sum(q[i].astype(F32) * n_ref[i][...], axis=1, keepdims=True) for i in C]
    for i in C:
        h_ref[i][...] = num[i] / jnp.maximum(jnp.abs(den[i]), jnp.exp(-m_j[i]))

    m_new = [jnp.maximum(gtot[i] + m_prev[i], m_loc[i]) for i in C]
    s_old = [jnp.exp(gtot[i] + m_prev[i] - m_new[i]) for i in C]
    s_new = [jnp.exp(m_loc[i] - m_new[i]) for i in C]
    for i in C:
        c_ref[i][...] = s_old[i] * c_ref[i][...] + s_new[i] * c_loc[i]
        n_ref[i][...] = s_old[i] * n_ref[i][...] + s_new[i] * n_loc[i]
        m_ref[i][...] = m_new[i]


def _mlstm_kernel(first_ref, qf_ref, kf_ref, vf_ref, ktf_ref, gtf_ref, qb_ref, kb_ref, vb_ref, ktb_ref,
                  gtb_ref, hf_ref, hb_ref, c_ref, n_ref, m_ref):
    i = pl.program_id(0)
    L = ML_CHUNK
    n_sub = qf_ref.shape[0] // L
    dirs = ((qf_ref, kf_ref, vf_ref, ktf_ref, gtf_ref, hf_ref),
            (qb_ref, kb_ref, vb_ref, ktb_ref, gtb_ref, hb_ref))
    for d in range(2):
        @pl.when(first_ref[d, i] == 1)
        def _():
            for hd in range(ML_HEADS):
                s = d * ML_HEADS + hd
                c_ref[s] = jnp.zeros(c_ref.shape[1:], F32)
                n_ref[s] = jnp.zeros(n_ref.shape[1:], F32)
                m_ref[s] = jnp.zeros(m_ref.shape[1:], F32)

    for sub in range(n_sub):
        for d, (q_ref, k_ref, v_ref, kt_ref, gt_ref, h_ref) in enumerate(dirs):
            rows = pl.ds((sub if d == 0 else n_sub - 1 - sub) * L, L)
            gt = gt_ref[:, rows]
            chains = []
            for hd in range(ML_HEADS):
                s = d * ML_HEADS + hd
                cols = slice(hd * ML_HEAD_DIM, (hd + 1) * ML_HEAD_DIM)
                chains.append((d, hd, q_ref[rows, cols], k_ref[rows, cols], kt_ref[cols, rows],
                               v_ref[rows, cols], gt, h_ref.at[rows, cols], c_ref.at[s], n_ref.at[s],
                               m_ref.at[s]))
            for g in range(0, ML_HEADS, ML_INTERLEAVE):
                _mlstm_chains(chains[g:g + ML_INTERLEAVE])


def _mlstm(rest, k_t, gates_t, seqs, n_sub):
    n = rest.shape[0]
    L = n_sub * ML_CHUNK
    nc = n // L
    first_f = np.zeros((nc,), np.int32)
    last_f = np.zeros((nc,), np.int32)
    tok = 0
    for (b, t) in seqs:
        assert t % L == 0
        for _ in range(b):
            first_f[tok // L] = 1
            last_f[(tok + t) // L - 1] = 1
            tok += t
    first = np.stack([first_f, last_f[::-1]]).astype(np.int32)
    W = ML_WIDTH
    nstate = 2 * ML_HEADS

    def fwd(c):
        return lambda i, fst: (i, c)

    def bwd(c):
        return lambda i, fst: (nc - 1 - i, c)

    def specs(m, mt):
        return [pl.BlockSpec((L, W), m(0)), pl.BlockSpec((L, W), m(1)), pl.BlockSpec((L, W), m(2)),
                pl.BlockSpec((W, L), mt), pl.BlockSpec((N_ML_GATES, L), mt)]

    grid_spec = pltpu.PrefetchScalarGridSpec(
        num_scalar_prefetch=1,
        grid=(nc,),
        in_specs=(specs(fwd, lambda i, fst: (0, i)) + specs(bwd, lambda i, fst: (0, nc - 1 - i))),
        out_specs=(pl.BlockSpec((L, W), fwd(0)), pl.BlockSpec((L, W), bwd(0))),
        scratch_shapes=[pltpu.VMEM((nstate, ML_HEAD_DIM, ML_HEAD_DIM), F32),
                        pltpu.VMEM((nstate, 1, ML_HEAD_DIM), F32),
                        pltpu.VMEM((nstate, 1, 1), F32)],
    )
    return pl.pallas_call(
        _mlstm_kernel,
        out_shape=(jax.ShapeDtypeStruct((n, W), F32), jax.ShapeDtypeStruct((n, W), F32)),
        grid_spec=grid_spec,
        compiler_params=_cparams(("arbitrary",), 32),
        name="mlstm",
    )(jnp.asarray(first), rest, rest, rest, k_t, gates_t, rest, rest, rest, k_t, gates_t)


def _merge_kernel(a_ref, hf_ref, hb_ref, mo_ref, ga_ref, gm_ref, wa_ref, wm_ref, wo_ref,
                  lng_ref, lnb_ref, *refs, alpha, split_block):
    x_refs, (xo_ref, xob_ref) = refs[:-2], refs[-2:]
    if split_block is None:
        x = x_refs[0][...]
    else:
        x = jnp.where(pl.program_id(0) < split_block, x_refs[0][...], x_refs[1][...])
    m_out = (_sigmoid(mo_ref[...].astype(F32)) * (hf_ref[...] + hb_ref[...])).astype(BF16)
    pa = jnp.dot(a_ref[...], wa_ref[...], preferred_element_type=F32)
    pm = jnp.dot(m_out, wm_ref[...], preferred_element_type=F32)
    merged = _sigmoid(ga_ref[...].astype(F32)) * pa + _sigmoid(gm_ref[...].astype(F32)) * pm
    y = jnp.dot(merged.astype(BF16), wo_ref[...], preferred_element_type=F32)
    z = _layer_norm(alpha * x + y, lng_ref[...], lnb_ref[...])
    xo_ref[...] = z
    xob_ref[...] = z.astype(BF16)


def _merge(a_out, hf, hb, rest, xs, wa, wm, wo, lng, lnb, alpha, tm):
    n, d = a_out.shape[0], wo.shape[1]
    assert n % tm == 0 and d == 2 * ML_WIDTH and ATT_WIDTH == ML_WIDTH
    const = dict(pipeline_mode=pl.Buffered(1))
    if len(xs) == 1:
        sb = None
        x_specs = [pl.BlockSpec((tm, d), lambda i: (i, 0))]
    else:
        assert len(xs) == 2 and xs[0].shape[0] % tm == 0 and xs[0].shape[0] + xs[1].shape[0] == n
        sb = xs[0].shape[0] // tm
        x_specs = [pl.BlockSpec((tm, d), lambda i: (jnp.minimum(i, sb - 1), 0)),
                   pl.BlockSpec((tm, d), lambda i: (jnp.maximum(i - sb, 0), 0))]
    return pl.pallas_call(
        functools.partial(_merge_kernel, alpha=alpha, split_block=sb),
        out_shape=(jax.ShapeDtypeStruct((n, d), F32), jax.ShapeDtypeStruct((n, d), BF16)),
        grid=(n // tm,),
        in_specs=[pl.BlockSpec((tm, ATT_WIDTH), lambda i: (i, 0)),
                  pl.BlockSpec((tm, ML_WIDTH), lambda i: (i, 0)),
                  pl.BlockSpec((tm, ML_WIDTH), lambda i: (i, 0)),
                  pl.BlockSpec((tm, ML_WIDTH), lambda i: (i, 3)),
                  pl.BlockSpec((tm, d), lambda i: (i, 2)),
                  pl.BlockSpec((tm, d), lambda i: (i, 3)),
                  pl.BlockSpec((ATT_WIDTH, d), lambda i: (0, 0), **const),
                  pl.BlockSpec((ML_WIDTH, d), lambda i: (0, 0), **const),
                  pl.BlockSpec((d, d), lambda i: (0, 0), **const),
                  pl.BlockSpec((1, d), lambda i: (0, 0)),
                  pl.BlockSpec((1, d), lambda i: (0, 0)),
                  *x_specs],
        out_specs=(pl.BlockSpec((tm, d), lambda i: (i, 0)),
                   pl.BlockSpec((tm, d), lambda i: (i, 0))),
        compiler_params=_cparams(("parallel",), 56),
        name="merge",
    )(a_out, hf, hb, rest, rest, rest, wa, wm, wo, lng, lnb, *xs)


def _ffn_kernel(xb_ref, w1_ref, w3_ref, w2_ref, x_ref, lng_ref, lnb_ref, *refs, alpha, n_cast):
    cast_in = refs[:n_cast]
    xo_ref, xob_ref = refs[n_cast:n_cast + 2]
    cast_out = refs[n_cast + 2:2 * n_cast + 2]
    acc_ref = refs[-1]
    j = pl.program_id(1)

    for src, dst in zip(cast_in, cast_out):
        dst[...] = src[...].astype(BF16)

    @pl.when(j == 0)
    def _():
        acc_ref[...] = jnp.zeros(acc_ref.shape, F32)

    xb = xb_ref[...]
    a = jnp.dot(xb, w1_ref[...], preferred_element_type=F32)
    b = jnp.dot(xb, w3_ref[...], preferred_element_type=F32)
    hid = (a * _sigmoid(a) * b).astype(BF16)
    acc_ref[...] += jnp.dot(hid, w2_ref[...], preferred_element_type=F32)

    @pl.when(j == pl.num_programs(1) - 1)
    def _():
        z = _layer_norm(alpha * x_ref[...] + acc_ref[...], lng_ref[...], lnb_ref[...])
        xo_ref[...] = z
        xob_ref[...] = z.astype(BF16)


def _cast_block_rows(rows, steps):
    rb = 16 * pl.cdiv(pl.cdiv(rows, steps), 16)
    while rows % rb:
        rb += 16
    return rb


def _ffn(xb, x, w1, w3, w2, lng, lnb, alpha, tm, tf, to_cast=()):
    n, d = x.shape
    f = w1.shape[1]
    assert n % tm == 0 and f % tf == 0
    nj = f // tf
    steps = (n // tm) * nj
    cast_specs = []
    for a in to_cast:
        rb = _cast_block_rows(a.shape[0], steps)
        last = a.shape[0] // rb - 1
        cast_specs.append(pl.BlockSpec((rb, a.shape[1]),
                                       lambda i, j, last=last: (jnp.minimum(i * nj + j, last), 0)))
    outs = pl.pallas_call(
        functools.partial(_ffn_kernel, alpha=alpha, n_cast=len(to_cast)),
        out_shape=(jax.ShapeDtypeStruct((n, d), F32), jax.ShapeDtypeStruct((n, d), BF16),
                   *[jax.ShapeDtypeStruct(a.shape, BF16) for a in to_cast]),
        grid=(n // tm, nj),
        in_specs=[pl.BlockSpec((tm, d), lambda i, j: (i, 0)),
                  pl.BlockSpec((d, tf), lambda i, j: (0, j)),
                  pl.BlockSpec((d, tf), lambda i, j: (0, j)),
                  pl.BlockSpec((tf, d), lambda i, j: (j, 0)),
                  pl.BlockSpec((tm, d), lambda i, j: (i, 0)),
                  pl.BlockSpec((1, d), lambda i, j: (0, 0)),
                  pl.BlockSpec((1, d), lambda i, j: (0, 0)),
                  *cast_specs],
        out_specs=(pl.BlockSpec((tm, d), lambda i, j: (i, 0)),
                   pl.BlockSpec((tm, d), lambda i, j: (i, 0)),
                   *cast_specs),
        scratch_shapes=[pltpu.VMEM((tm, d), F32)],
        compiler_params=_cparams(("arbitrary", "arbitrary"), 56),
        name="ffn_dense",
    )(xb, w1, w3, w2, x, lng, lnb, *to_cast)
    return outs[0], outs[1], outs[2:]


def _router_kernel(x_ref, wr_ref, tri_ref, ints_ref, flts_ref, cnt_ref, base_ref):
    i = pl.program_id(0)

    @pl.when(i == 0)
    def _():
        base_ref[...] = jnp.zeros(base_ref.shape, F32)

    logits = jnp.dot(x_ref[...], wr_ref[...], preferred_element_type=F32,
                     precision=lax.Precision.HIGHEST)
    lane = lax.broadcasted_iota(I32, logits.shape, 1)
    lanef = lane.astype(F32)
    logits = jnp.where(lane < N_EXPERTS, logits, -jnp.inf)
    v0 = jnp.max(logits, axis=1, keepdims=True)
    i0 = jnp.min(jnp.where(logits == v0, lanef, float(LANES)), axis=1, keepdims=True)
    rest = jnp.where(lanef == i0, -jnp.inf, logits)
    v1 = jnp.max(rest, axis=1, keepdims=True)
    i1 = jnp.min(jnp.where(rest == v1, lanef, float(LANES)), axis=1, keepdims=True)
    e1 = jnp.exp(v1 - v0)
    g0 = 1.0 / (1.0 + e1)
    g1 = e1 / (1.0 + e1)

    sel0 = lanef == i0
    sel1 = lanef == i1
    onehot = jnp.where(sel0 | sel1, 1.0, 0.0)
    before = jnp.dot(tri_ref[...], onehot.astype(BF16), preferred_element_type=F32) + base_ref[...]
    r0 = jnp.sum(jnp.where(sel0, before, 0.0), axis=1, keepdims=True)
    r1 = jnp.sum(jnp.where(sel1, before, 0.0), axis=1, keepdims=True)
    base_ref[...] += jnp.sum(onehot, axis=0, keepdims=True)
    cnt_ref[...] = jnp.broadcast_to(base_ref[...], cnt_ref.shape)

    ints = jnp.where(lane == 0, i0, jnp.where(lane == 1, i1, jnp.where(lane == 2, r0,
                     jnp.where(lane == 3, r1, 0.0))))
    ints_ref[...] = ints.astype(I32)
    flts_ref[...] = jnp.where(lane == 0, g0, jnp.where(lane == 1, g1, 0.0))


def _router(x, wr, tr):
    n, d = x.shape
    assert n % tr == 0
    tri = jnp.asarray(np.tril(np.ones((tr, tr), np.float32), -1), BF16)
    return pl.pallas_call(
        _router_kernel,
        out_shape=(jax.ShapeDtypeStruct((n, LANES), I32),
                   jax.ShapeDtypeStruct((n, LANES), F32),
                   jax.ShapeDtypeStruct((8, LANES), F32)),
        grid=(n // tr,),
        in_specs=[pl.BlockSpec((tr, d), lambda i: (i, 0)),
                  pl.BlockSpec((d, LANES), lambda i: (0, 0)),
                  pl.BlockSpec((tr, tr), lambda i: (0, 0))],
        out_specs=(pl.BlockSpec((tr, LANES), lambda i: (i, 0)),
                   pl.BlockSpec((tr, LANES), lambda i: (i, 0)),
                   pl.BlockSpec((8, LANES), lambda i: (0, 0))),
        scratch_shapes=[pltpu.VMEM((1, LANES), F32)],
        compiler_params=_cparams(("arbitrary",), 32),
        name="moe_router",
    )(x, wr, tri)


def _row_dma_start(make_copy, lo, hi):
    def issue(t, carry):
        for k in range(TOP_K):
            make_copy(t, k).start()
        return carry

    lax.fori_loop(lo, hi, issue, 0, unroll=8)


def _row_dma_wait(make_copy, lo, hi):
    def drain(t, carry):
        for k in range(TOP_K):
            make_copy(t, k).wait()
        return carry

    lax.fori_loop(lo, hi, drain, 0, unroll=8)


def _dispatch_kernel(pos_ref, x_ref, xs_in_ref, xs_ref, sem):
    del xs_in_ref

    def copy(t, k):
        return pltpu.make_async_copy(x_ref.at[pl.ds(t, 1), :],
                                     xs_ref.at[pl.ds(pos_ref[0, k, t], 1), :], sem)

    _row_dma_start(copy, 0, x_ref.shape[0])
    _row_dma_wait(copy, 0, x_ref.shape[0])


def _dispatch(x, pos, n_rows, tt):
    n, d = x.shape
    xs0 = jnp.zeros((n_rows, d), x.dtype)
    return pl.pallas_call(
        _dispatch_kernel,
        out_shape=jax.ShapeDtypeStruct(xs0.shape, xs0.dtype),
        grid=(n // tt,),
        in_specs=[pl.BlockSpec((1, TOP_K, tt), lambda i: (i, 0, 0), memory_space=pltpu.SMEM),
                  pl.BlockSpec((tt, d), lambda i: (i, 0)),
                  pl.BlockSpec(memory_space=pl.ANY)],
        out_specs=pl.BlockSpec(memory_space=pl.ANY),
        scratch_shapes=[pltpu.SemaphoreType.DMA],
        input_output_aliases={2: 0},
        compiler_params=_cparams(("arbitrary",), 32),
        name="moe_dispatch",
    )(pos, x, xs0)


def _moe_ffn_kernel(te_ref, nv_ref, xs_ref, w1_ref, w3_ref, w2_ref, y_ref, xb_ref):
    i = pl.program_id(0)
    j = pl.program_id(1)

    @pl.when(i < nv_ref[0])
    def _():
        @pl.when(j == 0)
        def _():
            xb_ref[...] = xs_ref[...].astype(BF16)
            y_ref[...] = jnp.zeros(y_ref.shape, F32)

        xb = xb_ref[...]
        a = jnp.dot(xb, w1_ref[...], preferred_element_type=F32)
        b = jnp.dot(xb, w3_ref[...], preferred_element_type=F32)
        hid = (a * _sigmoid(a) * b).astype(BF16)
        y_ref[...] += jnp.dot(hid, w2_ref[...], preferred_element_type=F32)

    @pl.when((i >= nv_ref[0]) & (j == 0))
    def _():
        y_ref[...] = jnp.zeros(y_ref.shape, F32)


def _moe_ffn(xs, tile_expert, n_valid, w1, w3, w2, tm, tf):
    p, d = xs.shape
    f = w1.shape[2]
    assert p % tm == 0 and f % tf == 0
    nj = f // tf

    def row_map(i, j, te, nv):
        return (jnp.minimum(i, nv[0] - 1), 0)

    def jj(i, j, nv):
        return jnp.where(i < nv[0], j, nj - 1)

    grid_spec = pltpu.PrefetchScalarGridSpec(
        num_scalar_prefetch=2,
        grid=(p // tm, nj),
        in_specs=[pl.BlockSpec((tm, d), row_map),
                  pl.BlockSpec((None, d, tf), lambda i, j, te, nv: (te[i], 0, jj(i, j, nv))),
                  pl.BlockSpec((None, d, tf), lambda i, j, te, nv: (te[i], 0, jj(i, j, nv))),
                  pl.BlockSpec((None, tf, d), lambda i, j, te, nv: (te[i], jj(i, j, nv), 0))],
        out_specs=pl.BlockSpec((tm, d), lambda i, j, te, nv: (i, 0)),
        scratch_shapes=[pltpu.VMEM((tm, d), BF16)],
    )
    return pl.pallas_call(
        _moe_ffn_kernel,
        out_shape=jax.ShapeDtypeStruct((p, d), F32),
        grid_spec=grid_spec,
        compiler_params=_cparams(("arbitrary", "arbitrary"), 56),
        name="moe_ffn",
    )(tile_expert, n_valid, xs, w1, w3, w2)


def _combine_kernel(pos_ref, y_ref, g_ref, x_ref, lng_ref, lnb_ref, *refs, alpha, split_block):
    out_refs = refs[:-3]
    buf0_ref, buf1_ref, sems = refs[-3:]
    tt = x_ref.shape[0]
    n_part = sems.shape[0]
    part = tt // n_part

    def copy(p):
        def make(t, k):
            buf = buf0_ref if k == 0 else buf1_ref
            return pltpu.make_async_copy(y_ref.at[pl.ds(pos_ref[0, k, t], 1), :], buf.at[pl.ds(t, 1), :],
                                         sems.at[p])
        return make

    for p in range(n_part):
        _row_dma_start(copy(p), p * part, (p + 1) * part)
    for p in range(n_part):
        _row_dma_wait(copy(p), p * part, (p + 1) * part)
        r = slice(p * part, (p + 1) * part)
        f = g_ref[r, 0:1] * buf0_ref[r, :] + g_ref[r, 1:2] * buf1_ref[r, :]
        z = _layer_norm(alpha * x_ref[r, :] + f, lng_ref[...], lnb_ref[...])
        if split_block is None:
            out_refs[0][r, :] = z
        else:
            @pl.when(pl.program_id(0) < split_block)
            def _():
                out_refs[0][r, :] = z

            @pl.when(pl.program_id(0) >= split_block)
            def _():
                out_refs[1][r, :] = z


def _combine(y, pos, gates, x, lng, lnb, alpha, tt, split=None):
    n, d = x.shape
    if split is None:
        sb = None
        out_shape = jax.ShapeDtypeStruct((n, d), F32)
        out_specs = pl.BlockSpec((tt, d), lambda i: (i, 0))
    else:
        assert split % tt == 0 and 0 < split < n
        sb = split // tt
        out_shape = (jax.ShapeDtypeStruct((split, d), F32), jax.ShapeDtypeStruct((n - split, d), F32))
        out_specs = (pl.BlockSpec((tt, d), lambda i: (jnp.minimum(i, sb - 1), 0)),
                     pl.BlockSpec((tt, d), lambda i: (jnp.maximum(i - sb, 0), 0)))
    return pl.pallas_call(
        functools.partial(_combine_kernel, alpha=alpha, split_block=sb),
        out_shape=out_shape,
        grid=(n // tt,),
        in_specs=[pl.BlockSpec((1, TOP_K, tt), lambda i: (i, 0, 0), memory_space=pltpu.SMEM),
                  pl.BlockSpec(memory_space=pl.ANY),
                  pl.BlockSpec((tt, LANES), lambda i: (i, 0)),
                  pl.BlockSpec((tt, d), lambda i: (i, 0)),
                  pl.BlockSpec((1, d), lambda i: (0, 0)),
                  pl.BlockSpec((1, d), lambda i: (0, 0))],
        out_specs=out_specs,
        scratch_shapes=[pltpu.VMEM((tt, d), F32), pltpu.VMEM((tt, d), F32),
                        pltpu.SemaphoreType.DMA((COMBINE_PARTS,))],
        compiler_params=_cparams(("arbitrary",), 48),
        name="moe_combine",
    )(pos, y, gates, x, lng, lnb)


def _moe(x, wr, w1, w3, w2, lng, lnb, alpha, tm, tf, tr, tt, split=None):
    n, d = x.shape
    ints, flts, cnt = _router(x, wr, tr)
    e0, e1, r0, r1 = ints[:, 0], ints[:, 1], ints[:, 2], ints[:, 3]
    counts = cnt[0, :N_EXPERTS].astype(I32)

    tiles = (counts + tm - 1) // tm
    tile_end = jnp.cumsum(tiles)
    offs = (tile_end - tiles) * tm
    n_tiles = (TOP_K * n) // tm + N_EXPERTS
    n_valid = tile_end[-1:]
    tile_expert = jnp.sum(jnp.arange(n_tiles, dtype=I32)[:, None] >= tile_end[None, :], axis=1)
    tile_expert = jnp.minimum(tile_expert, N_EXPERTS - 1).astype(I32)
    tile_expert = jnp.where(jnp.arange(n_tiles) < n_valid[0], tile_expert,
                            tile_expert[jnp.maximum(n_valid[0] - 1, 0)])
    pos = jnp.stack([offs[e0] + r0, offs[e1] + r1], axis=0)
    pos = pos.reshape(TOP_K, n // tt, tt).transpose(1, 0, 2)

    xs = _dispatch(x, pos, n_tiles * tm, tt)
    y = _moe_ffn(xs, tile_expert, n_valid.astype(I32), w1, w3, w2, tm, tf)
    return _combine(y, pos, flts, x, lng[None, :], lnb[None, :], alpha, tt, split=split)


def _rope_tables(t_max):
    half = ATT_HEAD_DIM // 2
    pos = np.arange(t_max)
    pos_r = (pos // GRID_W).astype(np.float32)
    pos_c = (pos % GRID_W).astype(np.float32)
    inv = jnp.asarray(ROPE_THETA, F32) ** (-jnp.arange(0, half, 2, dtype=F32) / half)
    ang_r = jnp.asarray(pos_r)[:, None] * inv
    ang_c = jnp.asarray(pos_c)[:, None] * inv
    cr, sr, cc, sc = jnp.cos(ang_r), jnp.sin(ang_r), jnp.cos(ang_c), jnp.sin(ang_c)
    zero = jnp.zeros_like(sr)
    cos = jnp.concatenate([cr, cr, cc, cc], axis=1)
    sin_up = jnp.concatenate([-sr, zero, -sc, zero], axis=1)
    sin_dn = jnp.concatenate([zero, sr, zero, sc], axis=1)
    return cos, sin_up, sin_dn


def _split_w_in(w_in_l):
    sizes = (ATT_WIDTH, KV_WIDTH, KV_WIDTH, ML_WIDTH, ML_WIDTH, ML_WIDTH, ML_WIDTH, N_ML_GATES)
    o = np.cumsum((0,) + sizes)
    d = w_in_l.shape[0]
    wb = lax.optimization_barrier(w_in_l.astype(BF16))
    w_gate = jnp.concatenate([wb[:, o[7]:o[8]], jnp.zeros((d, LANES - N_ML_GATES), BF16)], axis=1)
    w_att = jnp.concatenate([wb[:, o[0]:o[3]], w_gate], axis=1)
    w_rest = jnp.concatenate([wb[:, o[3]:o[7]], wb[:, o[8]:]], axis=1)
    return w_att, w_rest


def kernel(x_prompt, x_sample, w_in, b_gates, q_gain, k_gain, w_att_br, w_ml_br, w_o, ln1_g, ln1_b,
           w1_d, w3_d, w2_d, router, e_w1, e_w3, e_w2, ln2_g, ln2_b):
    depth = w_in.shape[0]
    d = x_prompt.shape[-1]
    seqs = ((x_prompt.shape[0], x_prompt.shape[1]), (x_sample.shape[0], x_sample.shape[1]))
    n0 = seqs[0][0] * seqs[0][1]
    alpha = float((2 * depth) ** 0.25)

    x = [x_prompt.reshape(-1, d), x_sample.reshape(-1, d)]
    xb = jnp.concatenate([x[0].astype(BF16), x[1].astype(BF16)], axis=0)
    cos, sin_up, sin_dn = _rope_tables(max(seqs[0][1], seqs[1][1]))

    for l in range(depth):
        w_att, w_rest = _split_w_in(w_in[l])
        bias = jnp.concatenate([b_gates[l], jnp.zeros((LANES - N_ML_GATES,), F32)])[None, :]
        gain = jnp.concatenate([jnp.broadcast_to(q_gain[l], (ATT_HEADS, ATT_HEAD_DIM)),
                                jnp.broadcast_to(k_gain[l], (ATT_KV_HEADS, ATT_HEAD_DIM)),
                                jnp.zeros((16 - ATT_HEADS - ATT_KV_HEADS, ATT_HEAD_DIM), F32)], axis=0)

        T = TILES
        qkv, gates = _att_proj(xb, w_att, bias, gain, cos, sin_up, sin_dn, seqs, tm=T["att_proj_tm"])
        rest = _matmul(xb, w_rest, BF16, tm=T["proj_tm"], tn=T["proj_tn"])

        a_out = jnp.concatenate(
            [_attention(qkv, 0, seqs[0][0], seqs[0][1], bq=min(T["attn_bq"], seqs[0][1]), tk=T["attn_tk"]),
             _attention(qkv, n0, seqs[1][0], seqs[1][1], bq=min(T["attn_bq"], seqs[1][1]), tk=T["attn_tk"])],
            axis=0)
        hf, hb = _mlstm(rest, rest[:, ML_WIDTH:2 * ML_WIDTH].T, gates[:, :N_ML_GATES].T, seqs,
                        n_sub=T["mlstm_chunks"])

        x, xb = _merge(a_out, hf, hb, rest, x if l == 0 else [x], w_att_br[l].astype(BF16), w_ml_br[l].astype(BF16),
                       w_o[l].astype(BF16), ln1_g[l][None, :], ln1_b[l][None, :], alpha, tm=T["merge_tm"])

        j = l // 2
        if l % 2 == 0:
            nxt = (e_w1[j], e_w3[j], e_w2[j]) if l + 1 < depth else ()
            x, xb, cast = _ffn(xb, x, w1_d[j].astype(BF16), w3_d[j].astype(BF16), w2_d[j].astype(BF16),
                               ln2_g[l][None, :], ln2_b[l][None, :], alpha, tm=T["ffn_tm"], tf=T["ffn_tf"],
                               to_cast=[e.reshape(-1, e.shape[-1]) for e in nxt])
            expert_w = [c.reshape(e.shape) for c, e in zip(cast, nxt)]
        else:
            wr = jnp.concatenate([router[j], jnp.zeros((d, LANES - N_EXPERTS), F32)], axis=1)
            last = l == depth - 1
            x = _moe(x, wr, *expert_w, ln2_g[l], ln2_b[l], alpha, tm=T["moe_tm"], tf=T["moe_tf"],
                     tr=T["router_tr"], tt=T["moe_tt"], split=n0 if last else None)
            if last:
                return (x[0].reshape(x_prompt.shape), x[1].reshape(x_sample.shape))
            xb = x.astype(BF16)

    return (x[:n0].reshape(x_prompt.shape), x[n0:].reshape(x_sample.shape))
```

```python
import functools
import math

import numpy as np
import jax
import jax.numpy as jnp
from jax import lax
from jax.experimental import pallas as pl
from jax.experimental.pallas import tpu as pltpu

F32 = jnp.float32
BF16 = jnp.bfloat16
I32 = jnp.int32

GRID_W = 64
ATT_HEADS = 8
ATT_KV_HEADS = 2
ATT_HEAD_DIM = 128
ATT_GROUP = ATT_HEADS // ATT_KV_HEADS
ROPE_THETA = 10000.0
QK_EPS = 1e-6
ML_HEADS = 4
ML_HEAD_DIM = 256
ML_CHUNK = 128
ML_INTERLEAVE = 2
COMBINE_PARTS = 2
ATT_WIDTH = ATT_HEADS * ATT_HEAD_DIM
KV_WIDTH = ATT_KV_HEADS * ATT_HEAD_DIM
ML_WIDTH = ML_HEADS * ML_HEAD_DIM
N_ML_GATES = 4 * ML_HEADS
N_EXPERTS = 8
TOP_K = 2
LN_EPS = 1e-5

LANES = 128
V7X_VMEM_BYTES = 64 * 2 ** 20
MIB = 2 ** 20

TILES = dict(att_proj_tm=512, proj_tm=1024, proj_tn=1024, attn_bq=1024, attn_tk=512, merge_tm=256,
             ffn_tm=512, ffn_tf=512, moe_tm=512, moe_tf=1024, router_tr=512, moe_tt=512, mlstm_chunks=2)


def _cparams(semantics, vmem_mib):
    assert vmem_mib * MIB < V7X_VMEM_BYTES
    return pltpu.CompilerParams(dimension_semantics=semantics, vmem_limit_bytes=vmem_mib * MIB)


def _sigmoid(x):
    return 1.0 / (1.0 + jnp.exp(-x))


def _log_sigmoid(x):
    return jnp.minimum(x, 0.0) - jnp.log1p(jnp.exp(-jnp.abs(x)))


def _mean_last(z, n_axes):
    count = 1
    for ax in range(z.ndim - 1, z.ndim - 1 - n_axes, -1):
        count *= z.shape[ax]
        z = jnp.sum(z, axis=ax, keepdims=True)
    return z * (1.0 / count)


def _layer_norm(z, g, b, n_axes=1):
    mu = _mean_last(z, n_axes)
    zc = z - mu
    var = _mean_last(zc * zc, n_axes)
    return zc * lax.rsqrt(var + LN_EPS) * g + b


def _att_proj_kernel(x_ref, w_ref, bias_ref, gain_ref, cos_ref, sa_ref, sb_ref,
                     qkv_ref, gates_ref, *, n_heads, n_norm_heads):
    x = x_ref[...]
    cos = cos_ref[...]
    sa = sa_ref[...]
    sb = sb_ref[...]
    ones = jnp.ones((x.shape[0], LANES), qkv_ref.dtype)
    for h in range(n_heads):
        if h % 2 == 0:
            pair = jnp.dot(x, w_ref[:, h * LANES:(h + 2) * LANES], preferred_element_type=F32)
        a = pair[:, (h % 2) * LANES:(h % 2 + 1) * LANES]
        if h < n_norm_heads:
            a = a * lax.rsqrt(jnp.mean(a * a, axis=-1, keepdims=True) + QK_EPS) * gain_ref[h:h + 1, :]
            a = a * cos + pltpu.roll(a, 96, 1) * sa + pltpu.roll(a, 32, 1) * sb
            qkv_ref[:, h * LANES:(h + 1) * LANES] = a.astype(qkv_ref.dtype)
        else:
            c = n_norm_heads + 2 * (h - n_norm_heads)
            qkv_ref[:, c * LANES:(c + 1) * LANES] = a.astype(qkv_ref.dtype)
            qkv_ref[:, (c + 1) * LANES:(c + 2) * LANES] = ones
    gates_ref[...] = (jnp.dot(x, w_ref[:, n_heads * LANES:], preferred_element_type=F32)
                      + bias_ref[...])


def _att_proj(xb, w, bias, gain, cos, sa, sb, seqs, tm):
    n, d = xb.shape
    n_heads = (ATT_WIDTH + 2 * KV_WIDTH) // LANES
    ncol = n_heads * LANES + LANES
    assert w.shape == (d, ncol)
    n_out = (n_heads + ATT_KV_HEADS) * LANES

    (b0, t0), (b1, t1) = seqs
    assert t0 % tm == 0 and t1 % tm == 0
    nb0 = b0 * t0 // tm

    def pos_map(i):
        return (jnp.where(i < nb0, i % (t0 // tm), (i - nb0) % (t1 // tm)), 0)

    return pl.pallas_call(
        functools.partial(_att_proj_kernel, n_heads=n_heads, n_norm_heads=ATT_HEADS + ATT_KV_HEADS),
        out_shape=(jax.ShapeDtypeStruct((n, n_out), BF16),
                   jax.ShapeDtypeStruct((n, LANES), F32)),
        grid=(n // tm,),
        in_specs=[pl.BlockSpec((tm, d), lambda i: (i, 0)),
                  pl.BlockSpec((d, ncol), lambda i: (0, 0)),
                  pl.BlockSpec((1, LANES), lambda i: (0, 0)),
                  pl.BlockSpec((16, LANES), lambda i: (0, 0)),
                  pl.BlockSpec((tm, LANES), pos_map),
                  pl.BlockSpec((tm, LANES), pos_map),
                  pl.BlockSpec((tm, LANES), pos_map)],
        out_specs=(pl.BlockSpec((tm, n_out), lambda i: (i, 0)),
                   pl.BlockSpec((tm, LANES), lambda i: (i, 0))),
        compiler_params=_cparams(("parallel",), 48),
        name="att_proj",
    )(xb, w, bias, gain, cos, sa, sb)


def _matmul_kernel(x_ref, w_ref, o_ref):
    o_ref[...] = jnp.dot(x_ref[...], w_ref[...], preferred_element_type=F32).astype(o_ref.dtype)


def _matmul(xb, w, out_dtype, tm, tn):
    n, d = xb.shape
    ncol = w.shape[1]
    assert n % tm == 0 and ncol % tn == 0
    return pl.pallas_call(
        _matmul_kernel,
        out_shape=jax.ShapeDtypeStruct((n, ncol), out_dtype),
        grid=(ncol // tn, n // tm),
        in_specs=[pl.BlockSpec((tm, d), lambda j, i: (i, 0)),
                  pl.BlockSpec((d, tn), lambda j, i: (0, j))],
        out_specs=pl.BlockSpec((tm, tn), lambda j, i: (i, j)),
        compiler_params=_cparams(("parallel", "parallel"), 48),
        name="matmul",
    )(xb, w)


def _attn_kernel(q_ref, k_ref, v_ref, o_ref, m_ref, acc_ref, *, tk, scale):
    bq = q_ref.shape[0]
    t = k_ref.shape[0]
    c = scale * math.log2(math.e)
    m_ref[...] = jnp.full(m_ref.shape, -jnp.inf, F32)
    acc_ref[...] = jnp.zeros(acc_ref.shape, F32)

    def body(ci, carry):
        start = pl.multiple_of(ci * tk, tk)
        k = k_ref[pl.ds(start, tk), :]
        v = v_ref[pl.ds(start, tk), :]
        for h in range(ATT_GROUP):
            r = slice(h * bq, (h + 1) * bq)
            s = lax.dot_general(q_ref[:, h * LANES:(h + 1) * LANES], k, (((1,), (1,)), ((), ())),
                                preferred_element_type=F32)
            m_prev = m_ref[r, :]
            m_new = jnp.maximum(m_prev, jnp.max(s, axis=1, keepdims=True))
            p = jnp.exp2((s - jnp.tile(m_new, (1, tk // LANES))) * c)
            alpha = jnp.exp2((m_prev - m_new) * c)
            pv = jnp.dot(p.astype(BF16), v, preferred_element_type=F32)
            acc_ref[r, :] = jnp.tile(alpha, (1, 2)) * acc_ref[r, :] + pv
            m_ref[r, :] = m_new
        return carry

    lax.fori_loop(0, t // tk, body, 0, unroll=2)
    for h in range(ATT_GROUP):
        r = slice(h * bq, (h + 1) * bq)
        o_ref[:, h * LANES:(h + 1) * LANES] = (acc_ref[r, :LANES] / acc_ref[r, LANES:]).astype(o_ref.dtype)


def _attention(qkv, tok0, nseq, t, bq, tk):
    assert t % bq == 0 and t % tk == 0 and tok0 % t == 0
    qb0 = tok0 // bq
    sb0 = tok0 // t
    nq = t // bq
    gw = ATT_GROUP * LANES
    v_blk0 = (ATT_HEADS + ATT_KV_HEADS) // 2
    return pl.pallas_call(
        functools.partial(_attn_kernel, tk=tk, scale=ATT_HEAD_DIM ** -0.5),
        out_shape=jax.ShapeDtypeStruct((nseq * t, ATT_WIDTH), BF16),
        grid=(nseq, ATT_KV_HEADS, nq),
        in_specs=[pl.BlockSpec((bq, gw), lambda b, g, i: (qb0 + b * nq + i, g)),
                  pl.BlockSpec((t, LANES), lambda b, g, i: (sb0 + b, ATT_HEADS + g)),
                  pl.BlockSpec((t, 2 * LANES), lambda b, g, i: (sb0 + b, v_blk0 + g))],
        out_specs=pl.BlockSpec((bq, gw), lambda b, g, i: (b * nq + i, g)),
        scratch_shapes=[pltpu.VMEM((ATT_GROUP * bq, LANES), F32),
                        pltpu.VMEM((ATT_GROUP * bq, 2 * LANES), F32)],
        compiler_params=_cparams(("parallel", "parallel", "parallel"), 40),
        name="attention",
    )(qkv, qkv, qkv)


def _mlstm_chains(chains):
    C = range(len(chains))
    ds = [c[0] for c in chains]
    q, k, kt, v, gt = ([c[i] for c in chains] for i in (2, 3, 4, 5, 6))
    h_ref, c_ref, n_ref, m_ref = ([c[i] for c in chains] for i in (7, 8, 9, 10))
    L, dk = q[0].shape
    ci = [c[0] * ML_HEADS + c[1] for c in chains]
    ig_row = [gt[i][ci[i]:ci[i] + 1, :] for i in C]
    logf_row = [_log_sigmoid(gt[i][2 * ML_HEADS + ci[i]:2 * ML_HEADS + ci[i] + 1, :]) for i in C]

    row = lax.broadcasted_iota(I32, (L, L), 0)
    col = lax.broadcasted_iota(I32, (L, L), 1)
    masks = (col <= row, col >= row)
    mask = [masks[ds[i]] for i in C]
    b_col = [jnp.sum(jnp.where(mask[i], logf_row[i], 0.0), axis=1, keepdims=True) for i in C]
    b_row = [jnp.sum(jnp.where(row == col, b_col[i], 0.0), axis=0, keepdims=True) for i in C]
    gtot = [jnp.sum(logf_row[i], axis=1, keepdims=True) for i in C]

    m_prev = [m_ref[i][...] for i in C]
    scale = dk ** -0.5
    assert math.log2(dk) % 2 == 0

    a_row = [gtot[i] - b_row[i] + ig_row[i] for i in C]
    m_loc = [jnp.max(a_row[i], axis=1, keepdims=True) for i in C]
    w_row = [jnp.exp(a_row[i] - m_loc[i]) * scale for i in C]
    c_loc = [jnp.dot((kt[i].astype(F32) * w_row[i]).astype(BF16), v[i], preferred_element_type=F32)
             for i in C]
    n_loc = [jnp.dot(jnp.broadcast_to(w_row[i], (8, L)).astype(BF16), k[i],
                     preferred_element_type=F32)[0:1, :] for i in C]

    dmat = [jnp.where(mask[i], b_col[i] - b_row[i] + ig_row[i], -jnp.inf) for i in C]
    inter = [b_col[i] + m_prev[i] for i in C]
    m_j = [jnp.maximum(jnp.max(dmat[i], axis=1, keepdims=True), inter[i]) for i in C]
    qk = [jnp.dot(q[i], kt[i], preferred_element_type=F32) for i in C]
    s = [qk[i] * (jnp.exp(dmat[i] - m_j[i]) * scale) for i in C]
    s_int = [jnp.exp(inter[i] - m_j[i]) for i in C]
    qc = [jnp.dot(q[i], c_ref[i][...].astype(BF16), preferred_element_type=F32) for i in C]
    num = [jnp.dot(s[i].astype(BF16), v[i], preferred_element_type=F32) + s_int[i] * qc[i] for i in C]
    den = [jnp.sum(s[i], axis=1, keepdims=True)
           + s_int[i] * jnp.sum(q[i].astype(F32) * n_ref[i][...], axis=1, keepdims=True) for i in C]
    for i in C:
        h_ref[i][...] = num[i] / jnp.maximum(jnp.abs(den[i]), jnp.exp(-m_j[i]))

    m_new = [jnp.maximum(gtot[i] + m_prev[i], m_loc[i]) for i in C]
    s_old = [jnp.exp(gtot[i] + m_prev[i] - m_new[i]) for i in C]
    s_new = [jnp.exp(m_loc[i] - m_new[i]) for i in C]
    for i in C:
        c_ref[i][...] = s_old[i] * c_ref[i][...] + s_new[i] * c_loc[i]
        n_ref[i][...] = s_old[i] * n_ref[i][...] + s_new[i] * n_loc[i]
        m_ref[i][...] = m_new[i]


def _mlstm_kernel(first_ref, qf_ref, kf_ref, vf_ref, ktf_ref, gtf_ref, qb_ref, kb_ref, vb_ref, ktb_ref,
                  gtb_ref, hf_ref, hb_ref, c_ref, n_ref, m_ref):
    i = pl.program_id(0)
    L = ML_CHUNK
    n_sub = qf_ref.shape[0] // L
    dirs = ((qf_ref, kf_ref, vf_ref, ktf_ref, gtf_ref, hf_ref),
            (qb_ref, kb_ref, vb_ref, ktb_ref, gtb_ref, hb_ref))
    for d in range(2):
        @pl.when(first_ref[d, i] == 1)
        def _():
            for hd in range(ML_HEADS):
                s = d * ML_HEADS + hd
                c_ref[s] = jnp.zeros(c_ref.shape[1:], F32)
                n_ref[s] = jnp.zeros(n_ref.shape[1:], F32)
                m_ref[s] = jnp.zeros(m_ref.shape[1:], F32)

    for sub in range(n_sub):
        for d, (q_ref, k_ref, v_ref, kt_ref, gt_ref, h_ref) in enumerate(dirs):
            rows = pl.ds((sub if d == 0 else n_sub - 1 - sub) * L, L)
            gt = gt_ref[:, rows]
            chains = []
            for hd in range(ML_HEADS):
                s = d * ML_HEADS + hd
                cols = slice(hd * ML_HEAD_DIM, (hd + 1) * ML_HEAD_DIM)
                chains.append((d, hd, q_ref[rows, cols], k_ref[rows, cols], kt_ref[cols, rows],
                               v_ref[rows, cols], gt, h_ref.at[rows, cols], c_ref.at[s], n_ref.at[s],
                               m_ref.at[s]))
            for g in range(0, ML_HEADS, ML_INTERLEAVE):
                _mlstm_chains(chains[g:g + ML_INTERLEAVE])


def _mlstm(rest, k_t, gates_t, seqs, n_sub):
    n = rest.shape[0]
    L = n_sub * ML_CHUNK
    nc = n // L
    first_f = np.zeros((nc,), np.int32)
    last_f = np.zeros((nc,), np.int32)
    tok = 0
    for (b, t) in seqs:
        assert t % L == 0
        for _ in range(b):
            first_f[tok // L] = 1
            last_f[(tok + t) // L - 1] = 1
            tok += t
    first = np.stack([first_f, last_f[::-1]]).astype(np.int32)
    W = ML_WIDTH
    nstate = 2 * ML_HEADS

    def fwd(c):
        return lambda i, fst: (i, c)

    def bwd(c):
        return lambda i, fst: (nc - 1 - i, c)

    def specs(m, mt):
        return [pl.BlockSpec((L, W), m(0)), pl.BlockSpec((L, W), m(1)), pl.BlockSpec((L, W), m(2)),
                pl.BlockSpec((W, L), mt), pl.BlockSpec((N_ML_GATES, L), mt)]

    grid_spec = pltpu.PrefetchScalarGridSpec(
        num_scalar_prefetch=1,
        grid=(nc,),
        in_specs=(specs(fwd, lambda i, fst: (0, i)) + specs(bwd, lambda i, fst: (0, nc - 1 - i))),
        out_specs=(pl.BlockSpec((L, W), fwd(0)), pl.BlockSpec((L, W), bwd(0))),
        scratch_shapes=[pltpu.VMEM((nstate, ML_HEAD_DIM, ML_HEAD_DIM), F32),
                        pltpu.VMEM((nstate, 1, ML_HEAD_DIM), F32),
                        pltpu.VMEM((nstate, 1, 1), F32)],
    )
    return pl.pallas_call(
        _mlstm_kernel,
        out_shape=(jax.ShapeDtypeStruct((n, W), F32), jax.ShapeDtypeStruct((n, W), F32)),
        grid_spec=grid_spec,
        compiler_params=_cparams(("arbitrary",), 32),
        name="mlstm",
    )(jnp.asarray(first), rest, rest, rest, k_t, gates_t, rest, rest, rest, k_t, gates_t)


def _merge_kernel(a_ref, hf_ref, hb_ref, mo_ref, ga_ref, gm_ref, wa_ref, wm_ref, wo_ref,
                  lng_ref, lnb_ref, *refs, alpha, split_block):
    x_refs, (xo_ref, xob_ref) = refs[:-2], refs[-2:]
    if split_block is None:
        x = x_refs[0][...]
    else:
        x = jnp.where(pl.program_id(0) < split_block, x_refs[0][...], x_refs[1][...])
    m_out = (_sigmoid(mo_ref[...].astype(F32)) * (hf_ref[...] + hb_ref[...])).astype(BF16)
    pa = jnp.dot(a_ref[...], wa_ref[...], preferred_element_type=F32)
    pm = jnp.dot(m_out, wm_ref[...], preferred_element_type=F32)
    merged = _sigmoid(ga_ref[...].astype(F32)) * pa + _sigmoid(gm_ref[...].astype(F32)) * pm
    y = jnp.dot(merged.astype(BF16), wo_ref[...], preferred_element_type=F32)
    z = _layer_norm(alpha * x + y, lng_ref[...], lnb_ref[...])
    xo_ref[...] = z
    xob_ref[...] = z.astype(BF16)


def _merge(a_out, hf, hb, rest, xs, wa, wm, wo, lng, lnb, alpha, tm):
    n, d = a_out.shape[0], wo.shape[1]
    assert n % tm == 0 and d == 2 * ML_WIDTH and ATT_WIDTH == ML_WIDTH
    const = dict(pipeline_mode=pl.Buffered(1))
    if len(xs) == 1:
        sb = None
        x_specs = [pl.BlockSpec((tm, d), lambda i: (i, 0))]
    else:
        assert len(xs) == 2 and xs[0].shape[0] % tm == 0 and xs[0].shape[0] + xs[1].shape[0] == n
        sb = xs[0].shape[0] // tm
        x_specs = [pl.BlockSpec((tm, d), lambda i: (jnp.minimum(i, sb - 1), 0)),
                   pl.BlockSpec((tm, d), lambda i: (jnp.maximum(i - sb, 0), 0))]
    return pl.pallas_call(
        functools.partial(_merge_kernel, alpha=alpha, split_block=sb),
        out_shape=(jax.ShapeDtypeStruct((n, d), F32), jax.ShapeDtypeStruct((n, d), BF16)),
        grid=(n // tm,),
        in_specs=[pl.BlockSpec((tm, ATT_WIDTH), lambda i: (i, 0)),
                  pl.BlockSpec((tm, ML_WIDTH), lambda i: (i, 0)),
                  pl.BlockSpec((tm, ML_WIDTH), lambda i: (i, 0)),
                  pl.BlockSpec((tm, ML_WIDTH), lambda i: (i, 3)),
                  pl.BlockSpec((tm, d), lambda i: (i, 2)),
                  pl.BlockSpec((tm, d), lambda i: (i, 3)),
                  pl.BlockSpec((ATT_WIDTH, d), lambda i: (0, 0), **const),
                  pl.BlockSpec((ML_WIDTH, d), lambda i: (0, 0), **const),
                  pl.BlockSpec((d, d), lambda i: (0, 0), **const),
                  pl.BlockSpec((1, d), lambda i: (0, 0)),
                  pl.BlockSpec((1, d), lambda i: (0, 0)),
                  *x_specs],
        out_specs=(pl.BlockSpec((tm, d), lambda i: (i, 0)),
                   pl.BlockSpec((tm, d), lambda i: (i, 0))),
        compiler_params=_cparams(("parallel",), 56),
        name="merge",
    )(a_out, hf, hb, rest, rest, rest, wa, wm, wo, lng, lnb, *xs)


def _ffn_kernel(xb_ref, w1_ref, w3_ref, w2_ref, x_ref, lng_ref, lnb_ref, *refs, alpha, n_cast):
    cast_in = refs[:n_cast]
    xo_ref, xob_ref = refs[n_cast:n_cast + 2]
    cast_out = refs[n_cast + 2:2 * n_cast + 2]
    acc_ref = refs[-1]
    j = pl.program_id(1)

    for src, dst in zip(cast_in, cast_out):
        dst[...] = src[...].astype(BF16)

    @pl.when(j == 0)
    def _():
        acc_ref[...] = jnp.zeros(acc_ref.shape, F32)

    xb = xb_ref[...]
    a = jnp.dot(xb, w1_ref[...], preferred_element_type=F32)
    b = jnp.dot(xb, w3_ref[...], preferred_element_type=F32)
    hid = (a * _sigmoid(a) * b).astype(BF16)
    acc_ref[...] += jnp.dot(hid, w2_ref[...], preferred_element_type=F32)

    @pl.when(j == pl.num_programs(1) - 1)
    def _():
        z = _layer_norm(alpha * x_ref[...] + acc_ref[...], lng_ref[...], lnb_ref[...])
        xo_ref[...] = z
        xob_ref[...] = z.astype(BF16)


def _cast_block_rows(rows, steps):
    rb = 16 * pl.cdiv(pl.cdiv(rows, steps), 16)
    while rows % rb:
        rb += 16
    return rb


def _ffn(xb, x, w1, w3, w2, lng, lnb, alpha, tm, tf, to_cast=()):
    n, d = x.shape
    f = w1.shape[1]
    assert n % tm == 0 and f % tf == 0
    nj = f // tf
    steps = (n // tm) * nj
    cast_specs = []
    for a in to_cast:
        rb = _cast_block_rows(a.shape[0], steps)
        last = a.shape[0] // rb - 1
        cast_specs.append(pl.BlockSpec((rb, a.shape[1]),
                                       lambda i, j, last=last: (jnp.minimum(i * nj + j, last), 0)))
    outs = pl.pallas_call(
        functools.partial(_ffn_kernel, alpha=alpha, n_cast=len(to_cast)),
        out_shape=(jax.ShapeDtypeStruct((n, d), F32), jax.ShapeDtypeStruct((n, d), BF16),
                   *[jax.ShapeDtypeStruct(a.shape, BF16) for a in to_cast]),
        grid=(n // tm, nj),
        in_specs=[pl.BlockSpec((tm, d), lambda i, j: (i, 0)),
                  pl.BlockSpec((d, tf), lambda i, j: (0, j)),
                  pl.BlockSpec((d, tf), lambda i, j: (0, j)),
                  pl.BlockSpec((tf, d), lambda i, j: (j, 0)),
                  pl.BlockSpec((tm, d), lambda i, j: (i, 0)),
                  pl.BlockSpec((1, d), lambda i, j: (0, 0)),
                  pl.BlockSpec((1, d), lambda i, j: (0, 0)),
                  *cast_specs],
        out_specs=(pl.BlockSpec((tm, d), lambda i, j: (i, 0)),
                   pl.BlockSpec((tm, d), lambda i, j: (i, 0)),
                   *cast_specs),
        scratch_shapes=[pltpu.VMEM((tm, d), F32)],
        compiler_params=_cparams(("arbitrary", "arbitrary"), 56),
        name="ffn_dense",
    )(xb, w1, w3, w2, x, lng, lnb, *to_cast)
    return outs[0], outs[1], outs[2:]


def _router_kernel(x_ref, wr_ref, tri_ref, ints_ref, flts_ref, cnt_ref, base_ref):
    i = pl.program_id(0)

    @pl.when(i == 0)
    def _():
        base_ref[...] = jnp.zeros(base_ref.shape, F32)

    logits = jnp.dot(x_ref[...], wr_ref[...], preferred_element_type=F32,
                     precision=lax.Precision.HIGHEST)
    lane = lax.broadcasted_iota(I32, logits.shape, 1)
    lanef = lane.astype(F32)
    logits = jnp.where(lane < N_EXPERTS, logits, -jnp.inf)
    v0 = jnp.max(logits, axis=1, keepdims=True)
    i0 = jnp.min(jnp.where(logits == v0, lanef, float(LANES)), axis=1, keepdims=True)
    rest = jnp.where(lanef == i0, -jnp.inf, logits)
    v1 = jnp.max(rest, axis=1, keepdims=True)
    i1 = jnp.min(jnp.where(rest == v1, lanef, float(LANES)), axis=1, keepdims=True)
    e1 = jnp.exp(v1 - v0)
    g0 = 1.0 / (1.0 + e1)
    g1 = e1 / (1.0 + e1)

    sel0 = lanef == i0
    sel1 = lanef == i1
    onehot = jnp.where(sel0 | sel1, 1.0, 0.0)
    before = jnp.dot(tri_ref[...], onehot.astype(BF16), preferred_element_type=F32) + base_ref[...]
    r0 = jnp.sum(jnp.where(sel0, before, 0.0), axis=1, keepdims=True)
    r1 = jnp.sum(jnp.where(sel1, before, 0.0), axis=1, keepdims=True)
    base_ref[...] += jnp.sum(onehot, axis=0, keepdims=True)
    cnt_ref[...] = jnp.broadcast_to(base_ref[...], cnt_ref.shape)

    ints = jnp.where(lane == 0, i0, jnp.where(lane == 1, i1, jnp.where(lane == 2, r0,
                     jnp.where(lane == 3, r1, 0.0))))
    ints_ref[...] = ints.astype(I32)
    flts_ref[...] = jnp.where(lane == 0, g0, jnp.where(lane == 1, g1, 0.0))


def _router(x, wr, tr):
    n, d = x.shape
    assert n % tr == 0
    tri = jnp.asarray(np.tril(np.ones((tr, tr), np.float32), -1), BF16)
    return pl.pallas_call(
        _router_kernel,
        out_shape=(jax.ShapeDtypeStruct((n, LANES), I32),
                   jax.ShapeDtypeStruct((n, LANES), F32),
                   jax.ShapeDtypeStruct((8, LANES), F32)),
        grid=(n // tr,),
        in_specs=[pl.BlockSpec((tr, d), lambda i: (i, 0)),
                  pl.BlockSpec((d, LANES), lambda i: (0, 0)),
                  pl.BlockSpec((tr, tr), lambda i: (0, 0))],
        out_specs=(pl.BlockSpec((tr, LANES), lambda i: (i, 0)),
                   pl.BlockSpec((tr, LANES), lambda i: (i, 0)),
                   pl.BlockSpec((8, LANES), lambda i: (0, 0))),
        scratch_shapes=[pltpu.VMEM((1, LANES), F32)],
        compiler_params=_cparams(("arbitrary",), 32),
        name="moe_router",
    )(x, wr, tri)


def _row_dma_start(make_copy, lo, hi):
    def issue(t, carry):
        for k in range(TOP_K):
            make_copy(t, k).start(priority=k % 2)
        return carry

    lax.fori_loop(lo, hi, issue, 0, unroll=8)


def _row_dma_wait(make_copy, lo, hi):
    def drain(t, carry):
        for k in range(TOP_K):
            make_copy(t, k).wait()
        return carry

    lax.fori_loop(lo, hi, drain, 0, unroll=8)


def _dispatch_kernel(pos_ref, x_ref, xs_in_ref, xs_ref, sem):
    del xs_in_ref

    def copy(t, k):
        return pltpu.make_async_copy(x_ref.at[pl.ds(t, 1), :],
                                     xs_ref.at[pl.ds(pos_ref[0, k, t], 1), :], sem)

    _row_dma_start(copy, 0, x_ref.shape[0])
    _row_dma_wait(copy, 0, x_ref.shape[0])


def _dispatch(x, pos, n_rows, tt):
    n, d = x.shape
    xs0 = jnp.zeros((n_rows, d), x.dtype)
    return pl.pallas_call(
        _dispatch_kernel,
        out_shape=jax.ShapeDtypeStruct(xs0.shape, xs0.dtype),
        grid=(n // tt,),
        in_specs=[pl.BlockSpec((1, TOP_K, tt), lambda i: (i, 0, 0), memory_space=pltpu.SMEM),
                  pl.BlockSpec((tt, d), lambda i: (i, 0)),
                  pl.BlockSpec(memory_space=pl.ANY)],
        out_specs=pl.BlockSpec(memory_space=pl.ANY),
        scratch_shapes=[pltpu.SemaphoreType.DMA],
        input_output_aliases={2: 0},
        compiler_params=_cparams(("arbitrary",), 32),
        name="moe_dispatch",
    )(pos, x, xs0)


def _moe_ffn_kernel(te_ref, nv_ref, xs_ref, w1_ref, w3_ref, w2_ref, y_ref, xb_ref):
    i = pl.program_id(0)
    j = pl.program_id(1)

    @pl.when(i < nv_ref[0])
    def _():
        @pl.when(j == 0)
        def _():
            xb_ref[...] = xs_ref[...].astype(BF16)
            y_ref[...] = jnp.zeros(y_ref.shape, F32)

        xb = xb_ref[...]
        a = jnp.dot(xb, w1_ref[...], preferred_element_type=F32)
        b = jnp.dot(xb, w3_ref[...], preferred_element_type=F32)
        hid = (a * _sigmoid(a) * b).astype(BF16)
        y_ref[...] += jnp.dot(hid, w2_ref[...], preferred_element_type=F32)

    @pl.when((i >= nv_ref[0]) & (j == 0))
    def _():
        y_ref[...] = jnp.zeros(y_ref.shape, F32)


def _moe_ffn(xs, tile_expert, n_valid, w1, w3, w2, tm, tf):
    p, d = xs.shape
    f = w1.shape[2]
    assert p % tm == 0 and f % tf == 0
    nj = f // tf

    def row_map(i, j, te, nv):
        return (jnp.minimum(i, nv[0] - 1), 0)

    def jj(i, j, nv):
        return jnp.where(i < nv[0], j, nj - 1)

    grid_spec = pltpu.PrefetchScalarGridSpec(
        num_scalar_prefetch=2,
        grid=(p // tm, nj),
        in_specs=[pl.BlockSpec((tm, d), row_map),
                  pl.BlockSpec((None, d, tf), lambda i, j, te, nv: (te[i], 0, jj(i, j, nv))),
                  pl.BlockSpec((None, d, tf), lambda i, j, te, nv: (te[i], 0, jj(i, j, nv))),
                  pl.BlockSpec((None, tf, d), lambda i, j, te, nv: (te[i], jj(i, j, nv), 0))],
        out_specs=pl.BlockSpec((tm, d), lambda i, j, te, nv: (i, 0)),
        scratch_shapes=[pltpu.VMEM((tm, d), BF16)],
    )
    return pl.pallas_call(
        _moe_ffn_kernel,
        out_shape=jax.ShapeDtypeStruct((p, d), F32),
        grid_spec=grid_spec,
        compiler_params=_cparams(("arbitrary", "arbitrary"), 56),
        name="moe_ffn",
    )(tile_expert, n_valid, xs, w1, w3, w2)


def _combine_kernel(pos_ref, y_ref, g_ref, x_ref, lng_ref, lnb_ref, *refs, alpha, split_block):
    out_refs = refs[:-3]
    buf0_ref, buf1_ref, sems = refs[-3:]
    tt = x_ref.shape[0]
    n_part = sems.shape[0]
    part = tt // n_part

    def copy(p):
        def make(t, k):
            buf = buf0_ref if k == 0 else buf1_ref
            return pltpu.make_async_copy(y_ref.at[pl.ds(pos_ref[0, k, t], 1), :], buf.at[pl.ds(t, 1), :],
                                         sems.at[p])
        return make

    for p in range(n_part):
        _row_dma_start(copy(p), p * part, (p + 1) * part)
    for p in range(n_part):
        _row_dma_wait(copy(p), p * part, (p + 1) * part)
        r = slice(p * part, (p + 1) * part)
        f = g_ref[r, 0:1] * buf0_ref[r, :] + g_ref[r, 1:2] * buf1_ref[r, :]
        z = _layer_norm(alpha * x_ref[r, :] + f, lng_ref[...], lnb_ref[...])
        if split_block is None:
            out_refs[0][r, :] = z
        else:
            @pl.when(pl.program_id(0) < split_block)
            def _():
                out_refs[0][r, :] = z

            @pl.when(pl.program_id(0) >= split_block)
            def _():
                out_refs[1][r, :] = z


def _combine(y, pos, gates, x, lng, lnb, alpha, tt, split=None):
    n, d = x.shape
    if split is None:
        sb = None
        out_shape = jax.ShapeDtypeStruct((n, d), F32)
        out_specs = pl.BlockSpec((tt, d), lambda i: (i, 0))
    else:
        assert split % tt == 0 and 0 < split < n
        sb = split // tt
        out_shape = (jax.ShapeDtypeStruct((split, d), F32), jax.ShapeDtypeStruct((n - split, d), F32))
        out_specs = (pl.BlockSpec((tt, d), lambda i: (jnp.minimum(i, sb - 1), 0)),
                     pl.BlockSpec((tt, d), lambda i: (jnp.maximum(i - sb, 0), 0)))
    return pl.pallas_call(
        functools.partial(_combine_kernel, alpha=alpha, split_block=sb),
        out_shape=out_shape,
        grid=(n // tt,),
        in_specs=[pl.BlockSpec((1, TOP_K, tt), lambda i: (i, 0, 0), memory_space=pltpu.SMEM),
                  pl.BlockSpec(memory_space=pl.ANY),
                  pl.BlockSpec((tt, LANES), lambda i: (i, 0)),
                  pl.BlockSpec((tt, d), lambda i: (i, 0)),
                  pl.BlockSpec((1, d), lambda i: (0, 0)),
                  pl.BlockSpec((1, d), lambda i: (0, 0))],
        out_specs=out_specs,
        scratch_shapes=[pltpu.VMEM((tt, d), F32), pltpu.VMEM((tt, d), F32),
                        pltpu.SemaphoreType.DMA((COMBINE_PARTS,))],
        compiler_params=_cparams(("arbitrary",), 48),
        name="moe_combine",
    )(pos, y, gates, x, lng, lnb)


def _moe(x, wr, w1, w3, w2, lng, lnb, alpha, tm, tf, tr, tt, split=None):
    n, d = x.shape
    ints, flts, cnt = _router(x, wr, tr)
    e0, e1, r0, r1 = ints[:, 0], ints[:, 1], ints[:, 2], ints[:, 3]
    counts = cnt[0, :N_EXPERTS].astype(I32)

    tiles = (counts + tm - 1) // tm
    tile_end = jnp.cumsum(tiles)
    offs = (tile_end - tiles) * tm
    n_tiles = (TOP_K * n) // tm + N_EXPERTS
    n_valid = tile_end[-1:]
    tile_expert = jnp.sum(jnp.arange(n_tiles, dtype=I32)[:, None] >= tile_end[None, :], axis=1)
    tile_expert = jnp.minimum(tile_expert, N_EXPERTS - 1).astype(I32)
    tile_expert = jnp.where(jnp.arange(n_tiles) < n_valid[0], tile_expert,
                            tile_expert[jnp.maximum(n_valid[0] - 1, 0)])
    pos = jnp.stack([offs[e0] + r0, offs[e1] + r1], axis=0)
    pos = pos.reshape(TOP_K, n // tt, tt).transpose(1, 0, 2)

    xs = _dispatch(x, pos, n_tiles * tm, tt)
    y = _moe_ffn(xs, tile_expert, n_valid.astype(I32), w1, w3, w2, tm, tf)
    return _combine(y, pos, flts, x, lng[None, :], lnb[None, :], alpha, tt, split=split)


def _rope_tables(t_max):
    half = ATT_HEAD_DIM // 2
    pos = np.arange(t_max)
    pos_r = (pos // GRID_W).astype(np.float32)
    pos_c = (pos % GRID_W).astype(np.float32)
    inv = jnp.asarray(ROPE_THETA, F32) ** (-jnp.arange(0, half, 2, dtype=F32) / half)
    ang_r = jnp.asarray(pos_r)[:, None] * inv
    ang_c = jnp.asarray(pos_c)[:, None] * inv
    cr, sr, cc, sc = jnp.cos(ang_r), jnp.sin(ang_r), jnp.cos(ang_c), jnp.sin(ang_c)
    zero = jnp.zeros_like(sr)
    cos = jnp.concatenate([cr, cr, cc, cc], axis=1)
    sin_up = jnp.concatenate([-sr, zero, -sc, zero], axis=1)
    sin_dn = jnp.concatenate([zero, sr, zero, sc], axis=1)
    return cos, sin_up, sin_dn


def _split_w_in(w_in_l):
    sizes = (ATT_WIDTH, KV_WIDTH, KV_WIDTH, ML_WIDTH, ML_WIDTH, ML_WIDTH, ML_WIDTH, N_ML_GATES)
    o = np.cumsum((0,) + sizes)
    d = w_in_l.shape[0]
    wb = lax.optimization_barrier(w_in_l.astype(BF16))
    w_gate = jnp.concatenate([wb[:, o[7]:o[8]], jnp.zeros((d, LANES - N_ML_GATES), BF16)], axis=1)
    w_att = jnp.concatenate([wb[:, o[0]:o[3]], w_gate], axis=1)
    w_rest = jnp.concatenate([wb[:, o[3]:o[7]], wb[:, o[8]:]], axis=1)
    return w_att, w_rest


def kernel(x_prompt, x_sample, w_in, b_gates, q_gain, k_gain, w_att_br, w_ml_br, w_o, ln1_g, ln1_b,
           w1_d, w3_d, w2_d, router, e_w1, e_w3, e_w2, ln2_g, ln2_b):
    depth = w_in.shape[0]
    d = x_prompt.shape[-1]
    seqs = ((x_prompt.shape[0], x_prompt.shape[1]), (x_sample.shape[0], x_sample.shape[1]))
    n0 = seqs[0][0] * seqs[0][1]
    alpha = float((2 * depth) ** 0.25)

    x = [x_prompt.reshape(-1, d), x_sample.reshape(-1, d)]
    xb = jnp.concatenate([x[0].astype(BF16), x[1].astype(BF16)], axis=0)
    cos, sin_up, sin_dn = _rope_tables(max(seqs[0][1], seqs[1][1]))

    for l in range(depth):
        w_att, w_rest = _split_w_in(w_in[l])
        bias = jnp.concatenate([b_gates[l], jnp.zeros((LANES - N_ML_GATES,), F32)])[None, :]
        gain = jnp.concatenate([jnp.broadcast_to(q_gain[l], (ATT_HEADS, ATT_HEAD_DIM)),
                                jnp.broadcast_to(k_gain[l], (ATT_KV_HEADS, ATT_HEAD_DIM)),
                                jnp.zeros((16 - ATT_HEADS - ATT_KV_HEADS, ATT_HEAD_DIM), F32)], axis=0)

        T = TILES
        qkv, gates = _att_proj(xb, w_att, bias, gain, cos, sin_up, sin_dn, seqs, tm=T["att_proj_tm"])
        rest = _matmul(xb, w_rest, BF16, tm=T["proj_tm"], tn=T["proj_tn"])

        a_out = jnp.concatenate(
            [_attention(qkv, 0, seqs[0][0], seqs[0][1], bq=min(T["attn_bq"], seqs[0][1]), tk=T["attn_tk"]),
             _attention(qkv, n0, seqs[1][0], seqs[1][1], bq=min(T["attn_bq"], seqs[1][1]), tk=T["attn_tk"])],
            axis=0)
        hf, hb = _mlstm(rest, rest[:, ML_WIDTH:2 * ML_WIDTH].T, gates[:, :N_ML_GATES].T, seqs,
                        n_sub=T["mlstm_chunks"])

        x, xb = _merge(a_out, hf, hb, rest, x if l == 0 else [x], w_att_br[l].astype(BF16), w_ml_br[l].astype(BF16),
                       w_o[l].astype(BF16), ln1_g[l][None, :], ln1_b[l][None, :], alpha, tm=T["merge_tm"])

        j = l // 2
        if l % 2 == 0:
            nxt = (e_w1[j], e_w3[j], e_w2[j]) if l + 1 < depth else ()
            x, xb, cast = _ffn(xb, x, w1_d[j].astype(BF16), w3_d[j].astype(BF16), w2_d[j].astype(BF16),
                               ln2_g[l][None, :], ln2_b[l][None, :], alpha, tm=T["ffn_tm"], tf=T["ffn_tf"],
                               to_cast=[e.reshape(-1, e.shape[-1]) for e in nxt])
            expert_w = [c.reshape(e.shape) for c, e in zip(cast, nxt)]
        else:
            wr = jnp.concatenate([router[j], jnp.zeros((d, LANES - N_EXPERTS), F32)], axis=1)
            last = l == depth - 1
            x = _moe(x, wr, *expert_w, ln2_g[l], ln2_b[l], alpha, tm=T["moe_tm"], tf=T["moe_tf"],
                     tr=T["router_tr"], tt=T["moe_tt"], split=n0 if last else None)
            if last:
                return (x[0].reshape(x_prompt.shape), x[1].reshape(x_sample.shape))
            xb = x.astype(BF16)

    return (x[:n0].reshape(x_prompt.shape), x[n0:].reshape(x_sample.shape))
```
